```python
import jax, jax.numpy as jnp
from jax import lax
import numpy as np


D_MODEL = 1024
BATCH = 32
SEQ = 2048
DEPTH = 4
DEC_BATCH = 32
DEC_SEQ = 32
PAST_LEN = 1024

CHUNK = 64
N_BRANCH = 3
D_A = 512
CONV_A_WIDTH = 31
D_SSM = 1024
SSM_HEAD_DIM = 64
SSM_HEADS = D_SSM // SSM_HEAD_DIM
SSM_GROUPS = 4
SSM_STATE = 128
CONV_B_WIDTH = 4
D_XBC = D_SSM + 2 * SSM_GROUPS * SSM_STATE
SSD_CHUNK = CHUNK
D_C = 512
GMLP_GROUPS = 4
GMLP_GROUP_DIM = D_C // GMLP_GROUPS
GMLP_CHUNK = 128
IN_SPLITS = (D_A, D_A, D_A, D_SSM, D_XBC, SSM_HEADS, D_C, D_C, D_C, N_BRANCH * D_MODEL)
D_IN = sum(IN_SPLITS)
RMS_EPS = 1e-6
LN_EPS = 1e-5

kernel_name = "hybrid_conv_ssd_gmlp_stream_step"


def _rmsnorm(x, g):
    xf = x.astype(jnp.float32)
    y = xf * lax.rsqrt(jnp.mean(xf * xf, axis=-1, keepdims=True) + RMS_EPS)
    return (y * g.astype(jnp.float32)).astype(x.dtype)


def _layernorm(x, g, b):
    xf = x.astype(jnp.float32)
    mu = jnp.mean(xf, axis=-1, keepdims=True)
    var = jnp.mean(jnp.square(xf - mu), axis=-1, keepdims=True)
    y = (xf - mu) * lax.rsqrt(var + LN_EPS)
    return (y * g.astype(jnp.float32) + b.astype(jnp.float32)).astype(x.dtype)


def _causal_dwconv(x, prev, w, b):
    width, ch = w.shape
    xp = jnp.concatenate([prev.astype(x.dtype), x], axis=1)
    y = lax.conv_general_dilated(xp, w[:, None, :].astype(x.dtype), window_strides=(1,),
                                 padding="VALID", dimension_numbers=("NWC", "WIO", "NWC"),
                                 feature_group_count=ch)
    return y + b.astype(x.dtype), xp[:, xp.shape[1] - (width - 1):]


def _ssd(x, dt, a, bm, cm, h0):
    bsz, L, H, P = x.shape
    G, N = bm.shape[2], bm.shape[3]
    R = H // G
    Q = L if L <= SSD_CHUNK else SSD_CHUNK
    nc = L // Q
    xf = x.astype(jnp.float32)
    xdt = (xf * dt[..., None]).reshape(bsz, nc, Q, G, R, P)
    da = (dt * a[None, None, :]).reshape(bsz, nc, Q, G, R).transpose(0, 3, 4, 1, 2)
    da_cs = jnp.cumsum(da, axis=-1)
    Bm = bm.astype(jnp.float32).reshape(bsz, nc, Q, G, N)
    Cm = cm.astype(jnp.float32).reshape(bsz, nc, Q, G, N)
    causal = jnp.tril(jnp.ones((Q, Q), bool))
    seg = da_cs[..., :, None] - da_cs[..., None, :]
    lmat = jnp.exp(jnp.where(causal, seg, -jnp.inf))
    cb = jnp.einsum('bclgn,bcsgn->bgcls', Cm, Bm)
    y_diag = jnp.einsum('bgcls,bgrcls,bcsgrp->bclgrp', cb, lmat, xdt)
    decay_states = jnp.exp(da_cs[..., -1:] - da_cs)
    states = jnp.einsum('bclgn,bgrcl,bclgrp->cbgrpn', Bm, decay_states, xdt)
    chunk_decay = jnp.moveaxis(jnp.exp(da_cs[..., -1]), -1, 0)

    def step(h, inp):
        s_c, d_c = inp
        return h * d_c[..., None, None] + s_c, h

    h_init = h0.astype(jnp.float32).reshape(bsz, G, R, P, N)
    h_final, h_enter = lax.scan(step, h_init, (states, chunk_decay))
    y_off = jnp.einsum('bclgn,cbgrpn,bgrcl->bclgrp', Cm, h_enter, jnp.exp(da_cs))
    y = (y_diag + y_off).reshape(bsz, L, H, P)
    return y, h_final.reshape(bsz, H, P, N)


def _spatial_gate(v, w_s, b_s):
    bsz, L, _ = v.shape
    n = -(-L // GMLP_CHUNK)
    pad = n * GMLP_CHUNK - L
    vp = jnp.pad(v, ((0, 0), (0, pad), (0, 0))).reshape(bsz, n, GMLP_CHUNK, GMLP_GROUPS, GMLP_GROUP_DIM)
    mask = jnp.tril(jnp.ones((GMLP_CHUNK, GMLP_CHUNK), bool))
    ws = jnp.where(mask[None], w_s, jnp.zeros_like(w_s))
    s = jnp.einsum('gts,bnsgc->bntgc', ws, vp) + b_s.T[None, None, :, :, None]
    return s.reshape(bsz, n * GMLP_CHUNK, D_C)[:, :L]


def _layer(x, c, conv_a_prev, conv_b_prev, ssm_prev, lp):
    bsz, L, _ = x.shape
    mod = jax.nn.silu(c) @ lp['w_ada'] + lp['b_ada']
    shift, scale, gate = jnp.split(mod[:, None, :], 3, axis=-1)
    h = _rmsnorm(x, lp['g_pre']) * (1 + scale) + shift
    proj = h @ lp['w_in']
    idx = [int(i) for i in np.cumsum(IN_SPLITS)[:-1]]
    a_val, a_glu, z_a, z_b, xbc, dt_raw, u_c, v_c, z_c, g_raw = jnp.split(proj, idx, axis=-1)

    a = a_val * jax.nn.sigmoid(a_glu)
    a_conv, conv_a_new = _causal_dwconv(a, conv_a_prev, lp['conv_a_w'], lp['conv_a_b'])
    a_out = jax.nn.silu(_layernorm(a_conv, lp['ln_a_g'], lp['ln_a_b']))
    o_a = (a_out * jax.nn.silu(z_a)) @ lp['w_pa']

    xbc_conv, conv_b_new = _causal_dwconv(xbc, conv_b_prev, lp['conv_b_w'], lp['conv_b_b'])
    xbc_conv = jax.nn.silu(xbc_conv)
    xs, bm, cm = jnp.split(xbc_conv, [D_SSM, D_SSM + SSM_GROUPS * SSM_STATE], axis=-1)
    xs = xs.reshape(bsz, L, SSM_HEADS, SSM_HEAD_DIM)
    bm = bm.reshape(bsz, L, SSM_GROUPS, SSM_STATE)
    cm = cm.reshape(bsz, L, SSM_GROUPS, SSM_STATE)
    dt = jax.nn.softplus(dt_raw.astype(jnp.float32) + lp['dt_bias'].astype(jnp.float32))
    a_h = -jnp.exp(lp['a_log'].astype(jnp.float32))
    y_ssm, ssm_new = _ssd(xs, dt, a_h, bm, cm, ssm_prev)
    y_ssm = y_ssm + xs.astype(jnp.float32) * lp['d_skip'].astype(jnp.float32)[:, None]
    y_ssm = y_ssm.reshape(bsz, L, D_SSM).astype(x.dtype)
    o_b = _rmsnorm(y_ssm * jax.nn.silu(z_b), lp['norm_b_g']) @ lp['w_pb']

    v_n = _layernorm(v_c, lp['ln_c_g'], lp['ln_c_b'])
    s = _spatial_gate(v_n, lp['w_s'], lp['b_s'])
    o_c = (u_c * s * jax.nn.silu(z_c)) @ lp['w_pc']

    g = jax.nn.sigmoid(g_raw + lp['b_merge'])
    g_a, g_b, g_c = jnp.split(g, 3, axis=-1)
    m = g_a * o_a + g_b * o_b + g_c * o_c
    out = m @ lp['w_o']
    x = x + gate * _rmsnorm(out, lp['g_post'])
    return x, conv_a_new, conv_b_new, ssm_new.astype(ssm_prev.dtype), v_n


def setup_inputs(seed: int = 0) -> dict:
    key = jax.random.key(seed)
    ks = iter(jax.random.split(key, 48))

    def nrm(shape, s):
        return jax.random.normal(next(ks), shape, jnp.float32) * s

    u = jax.random.uniform(next(ks), (DEPTH, SSM_HEADS), jnp.float32)
    dt0 = jnp.exp(u * (np.log(0.1) - np.log(0.001)) + np.log(0.001))
    dt_bias = dt0 + jnp.log(-jnp.expm1(-dt0))
    a_log = jnp.log(jax.random.uniform(next(ks), (DEPTH, SSM_HEADS), jnp.float32, 1.0, 16.0))
    return {
        'x_prompt': nrm((BATCH, SEQ, D_MODEL), 1.0),
        'x_sample': nrm((DEC_BATCH, DEC_SEQ, D_MODEL), 1.0),
        'cache_conv_a': nrm((DEPTH, DEC_BATCH, CONV_A_WIDTH - 1, D_A), 1.0),
        'cache_conv_b': nrm((DEPTH, DEC_BATCH, CONV_B_WIDTH - 1, D_XBC), 1.0),
        'state_ssm': nrm((DEPTH, DEC_BATCH, SSM_HEADS, SSM_HEAD_DIM, SSM_STATE), 0.5),
        'c_prompt': nrm((BATCH, D_MODEL), 1.0),
        'c_sample': nrm((DEC_BATCH, D_MODEL), 1.0),
        'w_ada': nrm((DEPTH, D_MODEL, 3 * D_MODEL), 0.5 * D_MODEL ** -0.5),
        'b_ada': nrm((DEPTH, 3 * D_MODEL), 0.01),
        'g_pre': 1.0 + nrm((DEPTH, D_MODEL), 0.01),
        'g_post': 1.0 + nrm((DEPTH, D_MODEL), 0.01),
        'w_in': nrm((DEPTH, D_MODEL, D_IN), D_MODEL ** -0.5),
        'b_merge': nrm((DEPTH, N_BRANCH * D_MODEL), 0.01),
        'conv_a_w': nrm((DEPTH, CONV_A_WIDTH, D_A), CONV_A_WIDTH ** -0.5),
        'conv_a_b': nrm((DEPTH, D_A), 0.01),
        'ln_a_g': 1.0 + nrm((DEPTH, D_A), 0.01),
        'ln_a_b': nrm((DEPTH, D_A), 0.01),
        'w_pa': nrm((DEPTH, D_A, D_MODEL), D_A ** -0.5),
        'conv_b_w': nrm((DEPTH, CONV_B_WIDTH, D_XBC), CONV_B_WIDTH ** -0.5),
        'conv_b_b': nrm((DEPTH, D_XBC), 0.01),
        'dt_bias': dt_bias,
        'a_log': a_log,
        'd_skip': 1.0 + nrm((DEPTH, SSM_HEADS), 0.1),
        'norm_b_g': 1.0 + nrm((DEPTH, D_SSM), 0.01),
        'w_pb': nrm((DEPTH, D_SSM, D_MODEL), D_SSM ** -0.5),
        'ln_c_g': 1.0 + nrm((DEPTH, D_C), 0.01),
        'ln_c_b': nrm((DEPTH, D_C), 0.01),
        'w_s': nrm((DEPTH, GMLP_GROUPS, GMLP_CHUNK, GMLP_CHUNK), GMLP_CHUNK ** -0.5),
        'b_s': 1.0 + nrm((DEPTH, GMLP_GROUPS, GMLP_CHUNK), 0.01),
        'w_pc': nrm((DEPTH, D_C, D_MODEL), D_C ** -0.5),
        'w_o': nrm((DEPTH, D_MODEL, D_MODEL), D_MODEL ** -0.5),
    }


def reference(x_prompt, x_sample, cache_conv_a, cache_conv_b, state_ssm, c_prompt, c_sample,
              w_ada, b_ada, g_pre, g_post, w_in, b_merge, conv_a_w, conv_a_b, ln_a_g, ln_a_b, w_pa,
              conv_b_w, conv_b_b, dt_bias, a_log, d_skip, norm_b_g, w_pb, ln_c_g, ln_c_b, w_s, b_s,
              w_pc, w_o):
    xp, xs = x_prompt, x_sample
    bp = x_prompt.shape[0]
    ca_p, cb_p, ss_p = [], [], []
    ca_s, cb_s, ss_s, v_s = [], [], [], []
    for l in range(DEPTH):
        lp = dict(w_ada=w_ada[l], b_ada=b_ada[l], g_pre=g_pre[l], g_post=g_post[l], w_in=w_in[l],
                  b_merge=b_merge[l], conv_a_w=conv_a_w[l], conv_a_b=conv_a_b[l], ln_a_g=ln_a_g[l],
                  ln_a_b=ln_a_b[l], w_pa=w_pa[l], conv_b_w=conv_b_w[l], conv_b_b=conv_b_b[l],
                  dt_bias=dt_bias[l], a_log=a_log[l], d_skip=d_skip[l], norm_b_g=norm_b_g[l],
                  w_pb=w_pb[l], ln_c_g=ln_c_g[l], ln_c_b=ln_c_b[l], w_s=w_s[l], b_s=b_s[l],
                  w_pc=w_pc[l], w_o=w_o[l])
        za = jnp.zeros((bp, CONV_A_WIDTH - 1, D_A), xp.dtype)
        zb = jnp.zeros((bp, CONV_B_WIDTH - 1, D_XBC), xp.dtype)
        zs = jnp.zeros((bp, SSM_HEADS, SSM_HEAD_DIM, SSM_STATE), state_ssm.dtype)
        xp, a1, b1, s1, _ = _layer(xp, c_prompt, za, zb, zs, lp)
        ca_p.append(a1); cb_p.append(b1); ss_p.append(s1)
        xs, a2, b2, s2, v2 = _layer(xs, c_sample, cache_conv_a[l], cache_conv_b[l], state_ssm[l], lp)
        ca_s.append(a2); cb_s.append(b2); ss_s.append(s2); v_s.append(v2)
    return (xp, xs, jnp.stack(ca_p), jnp.stack(cb_p), jnp.stack(ss_p),
            jnp.stack(ca_s), jnp.stack(cb_s), jnp.stack(ss_s), jnp.stack(v_s))
```

```python
import functools

import numpy as np
import jax
import jax.numpy as jnp
from jax import lax
from jax.experimental import pallas as pl
from jax.experimental.pallas import tpu as pltpu

D_MODEL = 1024
DEPTH = 4
D_A = 512
CONV_A_WIDTH = 31
D_SSM = 1024
HEAD_DIM = 64
HEADS = 16
GROUPS = 4
HEADS_PER_GROUP = HEADS // GROUPS
STATE = 128
CONV_B_WIDTH = 4
D_XBC = D_SSM + 2 * GROUPS * STATE
D_C = 512
GMLP_GROUPS = 4
GMLP_GROUP_DIM = D_C // GMLP_GROUPS
GMLP_CHUNK = 128
SSD_CHUNK = 64
N_BRANCH = 3
RMS_EPS = 1e-6
LN_EPS = 1e-5

LANES = 128
SUBLANES = 8

COL_A = 0
COL_ZB = COL_A + 3 * D_A
COL_XBC = COL_ZB + D_SSM
COL_C = COL_XBC + D_XBC
COL_G = COL_C + 3 * D_C
COL_DT = COL_G + N_BRANCH * D_MODEL
DT_PAD = LANES
W1_COLS = COL_DT + DT_PAD
DT_SPLIT = 3
SEG_K = 256

A_HIST = 32
B_HIST = 8
CONV_A_ROWS = 32
CONV_B_ROWS = 64

VMEM_LIMIT_BYTES = 58 * 1024 * 1024

F32 = jnp.float32
BF16 = jnp.bfloat16


def _dot(a, b):
    return jnp.dot(a, b, preferred_element_type=F32)


def _sigmoid(x):
    return 1.0 / (1.0 + jnp.exp(-x))


def _silu(x):
    return x * _sigmoid(x)


def _softplus(x):
    return jnp.maximum(x, 0.0) + jnp.log1p(jnp.exp(-jnp.abs(x)))


def _split3(x):
    p1 = x.astype(BF16)
    r1 = x - p1.astype(F32)
    p2 = r1.astype(BF16)
    r2 = r1 - p2.astype(F32)
    p3 = r2.astype(BF16)
    return p1, p2, p3


def _rms(x, g):
    ms = jnp.mean(x * x, axis=-1, keepdims=True)
    return x * lax.rsqrt(ms + RMS_EPS) * g


def _ln(x, g, b):
    mu = jnp.mean(x, axis=-1, keepdims=True)
    xc = x - mu
    var = jnp.mean(xc * xc, axis=-1, keepdims=True)
    return xc * lax.rsqrt(var + LN_EPS) * g + b


def _ada_kernel(c_ref, w_ref, b_ref, o_ref):
    c = _silu(c_ref[...]).astype(BF16)
    o_ref[...] = _dot(c, w_ref[...].astype(BF16)) + b_ref[...]


def _ada_mod(c_all, w_ada, b_ada):
    nb = c_all.shape[0]
    n_col = 3 * D_MODEL // D_MODEL
    return pl.pallas_call(
        _ada_kernel,
        grid=(DEPTH, n_col),
        in_specs=[
            pl.BlockSpec((nb, D_MODEL), lambda l, n: (0, 0)),
            pl.BlockSpec((None, D_MODEL, D_MODEL), lambda l, n: (l, 0, n)),
            pl.BlockSpec((None, 1, D_MODEL), lambda l, n: (l, 0, n)),
        ],
        out_specs=pl.BlockSpec((None, nb, D_MODEL), lambda l, n: (l, 0, n)),
        out_shape=jax.ShapeDtypeStruct((DEPTH, nb, 3 * D_MODEL), F32),
        name="ada_mod",
    )(c_all, w_ada, b_ada.reshape(DEPTH, 1, 3 * D_MODEL))


def _layer_kernel(*refs, T, Q, has_cache, emit_v):
    it = iter(refs)
    x_ref = next(it)
    mod_ref = next(it)
    if has_cache:
        ca_prev_ref = next(it)
        cb_prev_ref = next(it)
        ssm_prev_ref = next(it)
    w1_ref = next(it)
    wpa_ref = next(it)
    wpb_ref = next(it)
    wpc_ref = next(it)
    wo_ref = next(it)
    v1024_ref = next(it)
    bm_ref = next(it)
    v512_ref = next(it)
    caw_ref = next(it)
    cbw_ref = next(it)
    dtb_ref = next(it)
    ws_ref = next(it)
    bsf_ref = next(it)
    tri3_ref = next(it)
    um3_ref = next(it)
    lmask_ref = next(it)
    exp3_ref = next(it)
    y_ref = next(it)
    ca_new_ref = next(it)
    cb_new_ref = next(it)
    ssm_new_ref = next(it)
    if emit_v:
        vn_ref = next(it)
    exta, extb, cva, xbcc, dtbs, yss, st = it

    j = pl.program_id(1)
    last = pl.num_programs(1) - 1

    @pl.when(j == 0)
    def _init():
        if has_cache:
            exta[0:A_HIST - (CONV_A_WIDTH - 1), :] = jnp.zeros((A_HIST - (CONV_A_WIDTH - 1), D_A), F32)
            exta[A_HIST - (CONV_A_WIDTH - 1):A_HIST, :] = ca_prev_ref[...]
            extb[0:B_HIST - (CONV_B_WIDTH - 1), :] = jnp.zeros((B_HIST - (CONV_B_WIDTH - 1), D_XBC), F32)
            extb[B_HIST - (CONV_B_WIDTH - 1):B_HIST, :] = cb_prev_ref[...]
            st[...] = ssm_prev_ref[...].T
        else:
            exta[0:A_HIST, :] = jnp.zeros((A_HIST, D_A), F32)
            extb[0:B_HIST, :] = jnp.zeros((B_HIST, D_XBC), F32)
            st[...] = jnp.zeros((STATE, D_SSM), F32)

    x = x_ref[...]
    shift = mod_ref[0:1, :]
    scale = mod_ref[1:2, :]
    gate = mod_ref[2:3, :]
    h = _rms(x, v1024_ref[0:1, :]) * (1.0 + scale) + shift
    hb = h.astype(BF16)

    pa = _dot(hb, w1_ref[:, COL_A:COL_A + 3 * D_A])
    exta[A_HIST:A_HIST + T, :] = pa[:, 0:D_A] * _sigmoid(pa[:, D_A:2 * D_A])
    a_off = A_HIST - (CONV_A_WIDTH - 1)

    for r0 in range(0, T, CONV_A_ROWS):
        acc = jnp.broadcast_to(v512_ref[0:1, :], (CONV_A_ROWS, D_A))
        for k in range(CONV_A_WIDTH):
            acc = acc + caw_ref[k:k + 1, :] * exta[r0 + a_off + k:r0 + a_off + k + CONV_A_ROWS, :]
        cva[r0:r0 + CONV_A_ROWS, :] = acc
    tail_a = exta[T + a_off:T + A_HIST, :]
    exta[a_off:A_HIST, :] = tail_a

    @pl.when(j == last)
    def _():
        ca_new_ref[...] = tail_a

    a_out = _silu(_ln(cva[...], v512_ref[1:2, :], v512_ref[2:3, :]))
    o_a = _dot((a_out * _silu(pa[:, 2 * D_A:3 * D_A])).astype(BF16), wpa_ref[...])
    g_a = _sigmoid(_dot(hb, w1_ref[:, COL_G:COL_G + D_MODEL]) + bm_ref[:, 0:D_MODEL])
    m = g_a * o_a

    extb[B_HIST:B_HIST + T, :] = _dot(hb, w1_ref[:, COL_XBC:COL_XBC + D_XBC])
    b_off = B_HIST - (CONV_B_WIDTH - 1)

    rb = min(CONV_B_ROWS, T)
    for r0 in range(0, T, rb):
        acc = jnp.broadcast_to(cbw_ref[CONV_B_WIDTH:CONV_B_WIDTH + 1, :], (rb, D_XBC))
        for k in range(CONV_B_WIDTH):
            acc = acc + cbw_ref[k:k + 1, :] * extb[r0 + b_off + k:r0 + b_off + k + rb, :]
        xbcc[r0:r0 + rb, :] = _silu(acc)
    tail_b = extb[T + b_off:T + B_HIST, :]
    extb[b_off:B_HIST, :] = tail_b

    @pl.when(j == last)
    def _():
        cb_new_ref[...] = tail_b

    dt3 = _softplus(_dot(hb, w1_ref[:, COL_DT:COL_DT + DT_PAD]) + dtb_ref[...])
    p1, p2, p3 = _split3(dt3)
    lane = lax.broadcasted_iota(jnp.int32, (T, DT_PAD), 1)
    dsel = jnp.where(lane < HEADS, p1, jnp.where(lane < 2 * HEADS, p2, p3))
    dtbs[...] = _dot(dsel, exp3_ref[...])

    a_row = -jnp.exp(v1024_ref[4:5, :])
    left = lax.broadcasted_iota(jnp.int32, (Q, LANES), 1) < HEAD_DIM
    pair_pad = SSD_CHUNK - Q

    def chunk_body(c, carry):
        r0 = pl.multiple_of(c * Q, Q)
        xs = xbcc[pl.ds(r0, Q), 0:D_SSM]
        bmat = xbcc[pl.ds(r0, Q), D_SSM:D_SSM + GROUPS * STATE]
        cmat = xbcc[pl.ds(r0, Q), D_SSM + GROUPS * STATE:D_XBC]
        dtb = dtbs[pl.ds(r0, Q), :]
        dab = dtb * a_row
        xdt = xs * dtb
        pieces = list(_split3(dab))
        if SEG_K > DT_SPLIT * Q:
            pieces.append(jnp.zeros((SEG_K - DT_SPLIT * Q, D_SSM), BF16))
        pm = jnp.concatenate(pieces, axis=0)
        tri3 = tri3_ref[...]
        seg = _dot(tri3, pm * um3_ref[...])
        csb = _dot(tri3, pm)
        lm = jnp.exp(seg) * lmask_ref[...]
        ecs = jnp.exp(csb)
        clast = csb[Q - 1:Q, :]
        xdd = (xdt * jnp.exp(clast - csb)).astype(BF16)
        cdec = jnp.exp(clast)
        for g in range(GROUPS):
            gl = g * HEADS_PER_GROUP * HEAD_DIM
            gw = HEADS_PER_GROUP * HEAD_DIM
            bg = bmat[:, g * STATE:(g + 1) * STATE].astype(BF16)
            cg = cmat[:, g * STATE:(g + 1) * STATE].astype(BF16)
            if pair_pad:
                zrow = jnp.zeros((pair_pad, STATE), BF16)
                bg2 = jnp.concatenate([bg, zrow, bg, zrow], axis=0)
            else:
                bg2 = jnp.concatenate([bg, bg], axis=0)
            g2 = lax.dot_general(cg, bg2, (((1,), (1,)), ((), ())), preferred_element_type=F32)
            stg = st[:, gl:gl + gw]
            y_off = _dot(cg, stg.astype(BF16)) * ecs[:, gl:gl + gw]
            parts = []
            for pr in range(HEADS_PER_GROUP // 2):
                l0 = gl + pr * LANES
                mp = (g2 * lm[:, l0:l0 + LANES]).astype(BF16)
                xp = xdt[:, l0:l0 + LANES]
                top = jnp.where(left, xp, 0.0).astype(BF16)
                bot = jnp.where(left, 0.0, xp).astype(BF16)
                if pair_pad:
                    zpad = jnp.zeros((pair_pad, LANES), BF16)
                    rhs = jnp.concatenate([top, zpad, bot, zpad], axis=0)
                else:
                    rhs = jnp.concatenate([top, bot], axis=0)
                parts.append(_dot(mp, rhs))
            yss[pl.ds(r0, Q), gl:gl + gw] = jnp.concatenate(parts, axis=1) + y_off
            upd = lax.dot_general(bg, xdd[:, gl:gl + gw], (((0,), (0,)), ((), ())),
                                  preferred_element_type=F32)
            st[:, gl:gl + gw] = stg * cdec[:, gl:gl + gw] + upd
        return carry

    lax.fori_loop(0, T // Q, chunk_body, 0)

    @pl.when(j == last)
    def _():
        ssm_new_ref[...] = st[...].T

    y_ssm = yss[...] + xbcc[:, 0:D_SSM] * v1024_ref[3:4, :]
    zb = _dot(hb, w1_ref[:, COL_ZB:COL_ZB + D_SSM])
    o_b = _dot(_rms(y_ssm * _silu(zb), v1024_ref[2:3, :]).astype(BF16), wpb_ref[...])
    g_b = _sigmoid(_dot(hb, w1_ref[:, COL_G + D_MODEL:COL_G + 2 * D_MODEL]) + bm_ref[:, D_MODEL:2 * D_MODEL])
    m = m + g_b * o_b

    pc = _dot(hb, w1_ref[:, COL_C:COL_C + 3 * D_C])
    vn = _ln(pc[:, D_C:2 * D_C], v512_ref[3:4, :], v512_ref[4:5, :])
    if emit_v:
        vn_ref[...] = vn
    vnb = vn.astype(BF16)
    tc = min(T, GMLP_CHUNK)
    tri = (lax.broadcasted_iota(jnp.int32, (GMLP_CHUNK, GMLP_CHUNK), 0)
           >= lax.broadcasted_iota(jnp.int32, (GMLP_CHUNK, GMLP_CHUNK), 1))
    ws_b = [jnp.where(tri, ws_ref[g], 0.0).astype(BF16)[0:tc, :] for g in range(GMLP_GROUPS)]
    rows = []
    for cc in range(T // tc):
        cols = []
        for g in range(GMLP_GROUPS):
            vg = vnb[cc * tc:(cc + 1) * tc, g * GMLP_GROUP_DIM:(g + 1) * GMLP_GROUP_DIM]
            if tc < GMLP_CHUNK:
                vg = jnp.concatenate([vg, jnp.zeros((GMLP_CHUNK - tc, GMLP_GROUP_DIM), BF16)], axis=0)
            cols.append(_dot(ws_b[g], vg))
        rows.append(jnp.concatenate(cols, axis=1) + bsf_ref[0:tc, :])
    s = rows[0] if len(rows) == 1 else jnp.concatenate(rows, axis=0)
    o_c = _dot((pc[:, 0:D_C] * s * _silu(pc[:, 2 * D_C:3 * D_C])).astype(BF16), wpc_ref[...])
    g_c = _sigmoid(_dot(hb, w1_ref[:, COL_G + 2 * D_MODEL:COL_G + 3 * D_MODEL]) + bm_ref[:, 2 * D_MODEL:3 * D_MODEL])
    m = m + g_c * o_c

    out = _dot(m.astype(BF16), wo_ref[...])
    y_ref[...] = x + gate * _rms(out, v1024_ref[1:2, :])


def _ssd_constants(Q):
    l = np.arange(Q)[:, None]
    t = np.arange(Q)[None, :]
    tri = (t <= l).astype(np.float32)
    kpad = SEG_K - DT_SPLIT * Q
    tri3 = np.concatenate([tri] * DT_SPLIT + [np.zeros((Q, kpad), np.float32)], axis=1)
    s_of_lane = (np.arange(D_SSM) % HEAD_DIM)[None, :]
    um = (np.arange(Q)[:, None] > s_of_lane).astype(np.float32)
    um3 = np.concatenate([um] * DT_SPLIT + [np.zeros((kpad, D_SSM), np.float32)], axis=0)
    lmask = (np.arange(Q)[:, None] >= s_of_lane).astype(np.float32)
    lmask = lmask * (s_of_lane < Q)
    head_of_lane = np.arange(D_SSM) // HEAD_DIM
    exp3 = np.zeros((DT_PAD, D_SSM), np.float32)
    for i in range(DT_SPLIT):
        exp3[i * HEADS + head_of_lane, np.arange(D_SSM)] = 1.0
    return (jnp.asarray(tri3, BF16), jnp.asarray(um3, BF16),
            jnp.asarray(lmask, F32), jnp.asarray(exp3, BF16))


def _resident(shape, index_map):
    return pl.BlockSpec(shape, index_map, pipeline_mode=pl.Buffered(1))


def _layer_call(l, x, mod, mod_off, caches, packed, *, T, Q, emit_v):
    B, L, _ = x.shape
    has_cache = caches is not None
    nT = L // T
    (w1, wpa, wpb, wpc, wo, v1024, bm, v512, caw, cbw, dtb, ws, bsf) = packed
    consts = _ssd_constants(Q)

    def lay(*tail):
        return lambda b, j: (l,) + tail

    in_specs = [
        pl.BlockSpec((None, T, D_MODEL), lambda b, j: (b, j, 0)),
        pl.BlockSpec((None, None, 3, D_MODEL), lambda b, j: (l, mod_off + b, 0, 0)),
    ]
    args = [x, mod]
    if has_cache:
        ca_prev, cb_prev, ssm_prev = caches
        in_specs += [
            pl.BlockSpec((None, None, CONV_A_WIDTH - 1, D_A), lambda b, j: (l, b, 0, 0)),
            pl.BlockSpec((None, None, CONV_B_WIDTH - 1, D_XBC), lambda b, j: (l, b, 0, 0)),
            pl.BlockSpec((None, None, D_SSM, STATE), lambda b, j: (l, b, 0, 0)),
        ]
        args += [ca_prev, cb_prev, ssm_prev]
    in_specs += [
        _resident((None, D_MODEL, W1_COLS), lay(0, 0)),
        _resident((None, D_A, D_MODEL), lay(0, 0)),
        _resident((None, D_SSM, D_MODEL), lay(0, 0)),
        _resident((None, D_C, D_MODEL), lay(0, 0)),
        _resident((None, D_MODEL, D_MODEL), lay(0, 0)),
        _resident((None, SUBLANES, D_MODEL), lay(0, 0)),
        _resident((None, 1, N_BRANCH * D_MODEL), lay(0, 0)),
        _resident((None, SUBLANES, D_A), lay(0, 0)),
        _resident((None, 32, D_A), lay(0, 0)),
        _resident((None, SUBLANES, D_XBC), lay(0, 0)),
        _resident((None, 1, DT_PAD), lay(0, 0)),
        _resident((None, GMLP_GROUPS, GMLP_CHUNK, GMLP_CHUNK), lay(0, 0, 0)),
        _resident((None, GMLP_CHUNK, D_C), lay(0, 0)),
    ]
    args += [w1, wpa, wpb, wpc, wo, v1024, bm, v512, caw, cbw, dtb, ws, bsf]
    for cst in consts:
        in_specs.append(_resident(cst.shape, lambda b, j: (0, 0)))
        args.append(cst)

    out_specs = [
        pl.BlockSpec((None, T, D_MODEL), lambda b, j: (b, j, 0)),
        pl.BlockSpec((None, CONV_A_WIDTH - 1, D_A), lambda b, j: (b, 0, 0)),
        pl.BlockSpec((None, CONV_B_WIDTH - 1, D_XBC), lambda b, j: (b, 0, 0)),
        pl.BlockSpec((None, D_SSM, STATE), lambda b, j: (b, 0, 0)),
    ]
    out_shape = [
        jax.ShapeDtypeStruct((B, L, D_MODEL), F32),
        jax.ShapeDtypeStruct((B, CONV_A_WIDTH - 1, D_A), F32),
        jax.ShapeDtypeStruct((B, CONV_B_WIDTH - 1, D_XBC), F32),
        jax.ShapeDtypeStruct((B, D_SSM, STATE), F32),
    ]
    if emit_v:
        out_specs.append(pl.BlockSpec((None, T, D_C), lambda b, j: (b, j, 0)))
        out_shape.append(jax.ShapeDtypeStruct((B, L, D_C), F32))

    scratch = [
        pltpu.VMEM((A_HIST + T, D_A), F32),
        pltpu.VMEM((B_HIST + T, D_XBC), F32),
        pltpu.VMEM((T, D_A), F32),
        pltpu.VMEM((T, D_XBC), F32),
        pltpu.VMEM((T, D_SSM), F32),
        pltpu.VMEM((T, D_SSM), F32),
        pltpu.VMEM((STATE, D_SSM), F32),
    ]
    return pl.pallas_call(
        functools.partial(_layer_kernel, T=T, Q=Q, has_cache=has_cache, emit_v=emit_v),
        grid=(B, nT),
        in_specs=in_specs,
        out_specs=out_specs,
        out_shape=out_shape,
        scratch_shapes=scratch,
        compiler_params=pltpu.CompilerParams(
            dimension_semantics=("arbitrary", "arbitrary"),
            vmem_limit_bytes=VMEM_LIMIT_BYTES),
        name="layer_cached" if has_cache else "layer_fresh",
    )(*args)


def _pack_params(w_in, b_merge, g_pre, g_post, norm_b_g, d_skip, a_log, conv_a_w, conv_a_b, ln_a_g,
                 ln_a_b, ln_c_g, ln_c_b, conv_b_w, conv_b_b, dt_bias, w_pa, w_pb, w_pc, w_o, w_s, b_s):
    splits = np.cumsum([0, D_A, D_A, D_A, D_SSM, D_XBC, HEADS, D_C, D_C, D_C, N_BRANCH * D_MODEL])
    seg = lambda i, k: w_in[:, :, int(splits[i]):int(splits[k])]
    w_dt = seg(5, 6)
    w1 = jnp.concatenate(
        [seg(0, 3), seg(3, 4), seg(4, 5), seg(6, 9), seg(9, 10)] + [w_dt] * DT_SPLIT
        + [jnp.zeros((DEPTH, D_MODEL, DT_PAD - DT_SPLIT * HEADS), F32)], axis=-1).astype(BF16)

    def table(rows, width, n_rows):
        t = jnp.stack(rows, axis=1)
        return jnp.pad(t, ((0, 0), (0, n_rows - len(rows)), (0, 0)))

    rep = lambda v: jnp.repeat(v, HEAD_DIM, axis=-1)
    v1024 = table([g_pre, g_post, norm_b_g, rep(d_skip), rep(a_log)], D_MODEL, SUBLANES)
    v512 = table([conv_a_b, ln_a_g, ln_a_b, ln_c_g, ln_c_b], D_A, SUBLANES)
    caw = jnp.pad(conv_a_w, ((0, 0), (0, 32 - CONV_A_WIDTH), (0, 0)))
    cbw = jnp.pad(jnp.concatenate([conv_b_w, conv_b_b[:, None, :]], axis=1),
                  ((0, 0), (0, SUBLANES - CONV_B_WIDTH - 1), (0, 0)))
    dtb = jnp.pad(jnp.concatenate([dt_bias] * DT_SPLIT, axis=-1),
                  ((0, 0), (0, DT_PAD - DT_SPLIT * HEADS)))[:, None, :]
    bsf = jnp.repeat(jnp.swapaxes(b_s, 1, 2), GMLP_GROUP_DIM, axis=-1)
    return (w1, w_pa.astype(BF16), w_pb.astype(BF16), w_pc.astype(BF16), w_o.astype(BF16),
            v1024, b_merge[:, None, :], v512, caw, cbw, dtb, w_s, bsf)


def kernel(x_prompt, x_sample, cache_conv_a, cache_conv_b, state_ssm, c_prompt, c_sample, w_ada, b_ada, g_pre, g_post, w_in, b_merge, conv_a_w, conv_a_b, ln_a_g, ln_a_b, w_pa, conv_b_w, conv_b_b, dt_bias, a_log, d_skip, norm_b_g, w_pb, ln_c_g, ln_c_b, w_s, b_s, w_pc, w_o):
    bp = x_prompt.shape[0]
    bs, ls = x_sample.shape[0], x_sample.shape[1]
    mod = _ada_mod(jnp.concatenate([c_prompt, c_sample], axis=0), w_ada, b_ada)
    mod = mod.reshape(DEPTH, bp + bs, 3, D_MODEL)
    packed = _pack_params(w_in, b_merge, g_pre, g_post, norm_b_g, d_skip, a_log, conv_a_w, conv_a_b,
                          ln_a_g, ln_a_b, ln_c_g, ln_c_b, conv_b_w, conv_b_b, dt_bias, w_pa, w_pb,
                          w_pc, w_o, w_s, b_s)
    caches = (cache_conv_a, cache_conv_b,
              state_ssm.reshape(DEPTH, bs, D_SSM, STATE))

    xp, xs = x_prompt, x_sample
    outs_p, outs_s = [], []
    for l in range(DEPTH):
        xp, ca, cb, ss = _layer_call(l, xp, mod, 0, None, packed, T=PROMPT_TILE, Q=SSD_CHUNK, emit_v=False)
        outs_p.append((ca, cb, ss))
        xs, ca, cb, ss, vn = _layer_call(l, xs, mod, bp, caches, packed, T=ls, Q=min(ls, SSD_CHUNK), emit_v=True)
        outs_s.append((ca, cb, ss, vn))

    stack = lambda outs, i: jnp.stack([o[i] for o in outs])
    ssm_shape = lambda b: (DEPTH, b, HEADS, HEAD_DIM, STATE)
    return (xp, xs,
            stack(outs_p, 0), stack(outs_p, 1), stack(outs_p, 2).reshape(ssm_shape(bp)),
            stack(outs_s, 0), stack(outs_s, 1), stack(outs_s, 2).reshape(ssm_shape(bs)),
            stack(outs_s, 3))


PROMPT_TILE = 256
```

```python
import functools

import numpy as np
import jax
import jax.numpy as jnp
from jax import lax
from jax.experimental import pallas as pl
from jax.experimental.pallas import tpu as pltpu

D_MODEL = 1024
DEPTH = 4
D_A = 512
CONV_A_WIDTH = 31
D_SSM = 1024
HEAD_DIM = 64
HEADS = 16
GROUPS = 4
HEADS_PER_GROUP = HEADS // GROUPS
STATE = 128
CONV_B_WIDTH = 4
D_XBC = D_SSM + 2 * GROUPS * STATE
D_C = 512
GMLP_GROUPS = 4
GMLP_GROUP_DIM = D_C // GMLP_GROUPS
GMLP_CHUNK = 128
SSD_CHUNK = 64
N_BRANCH = 3
RMS_EPS = 1e-6
LN_EPS = 1e-5

LANES = 128
SUBLANES = 8

COL_A = 0
COL_ZB = COL_A + 3 * D_A
COL_XBC = COL_ZB + D_SSM
COL_C = COL_XBC + D_XBC
COL_G = COL_C + 3 * D_C
COL_DT = COL_G + N_BRANCH * D_MODEL
DT_PAD = LANES
W1_COLS = COL_DT + DT_PAD
DT_SPLIT = 3
SEG_K = 256

A_HIST = 32
B_HIST = 8
A_SLABS = D_A // LANES
B_SLABS = D_XBC // LANES
X_SLABS = D_SSM // LANES
PROJ_BLOCK = 512
A_BLOCK = 256

VMEM_LIMIT_BYTES = 58 * 1024 * 1024

F32 = jnp.float32
BF16 = jnp.bfloat16


def _dot(a, b):
    return jnp.dot(a, b, preferred_element_type=F32)


def _sigmoid(x):
    return 1.0 / (1.0 + jnp.exp(-x))


def _silu(x):
    return x * _sigmoid(x)


def _softplus(x):
    return jnp.maximum(x, 0.0) + jnp.log1p(jnp.exp(-jnp.abs(x)))


def _split3(x):
    p1 = x.astype(BF16)
    r1 = x - p1.astype(F32)
    p2 = r1.astype(BF16)
    r2 = r1 - p2.astype(F32)
    p3 = r2.astype(BF16)
    return p1, p2, p3


def _rms(x, g):
    ms = jnp.mean(x * x, axis=-1, keepdims=True)
    return x * lax.rsqrt(ms + RMS_EPS) * g


def _ln(x, g, b):
    mu = jnp.mean(x, axis=-1, keepdims=True)
    xc = x - mu
    var = jnp.mean(xc * xc, axis=-1, keepdims=True)
    return xc * lax.rsqrt(var + LN_EPS) * g + b


def _ada_kernel(c_ref, w_ref, b_ref, o_ref):
    c = _silu(c_ref[...]).astype(BF16)
    o_ref[...] = _dot(c, w_ref[...].astype(BF16)) + b_ref[...]


def _ada_mod(c_all, w_ada, b_ada):
    nb = c_all.shape[0]
    n_col = 3 * D_MODEL // D_MODEL
    return pl.pallas_call(
        _ada_kernel,
        grid=(DEPTH, n_col),
        in_specs=[
            pl.BlockSpec((nb, D_MODEL), lambda l, n: (0, 0)),
            pl.BlockSpec((None, D_MODEL, D_MODEL), lambda l, n: (l, 0, n)),
            pl.BlockSpec((None, 1, D_MODEL), lambda l, n: (l, 0, n)),
        ],
        out_specs=pl.BlockSpec((None, nb, D_MODEL), lambda l, n: (l, 0, n)),
        out_shape=jax.ShapeDtypeStruct((DEPTH, nb, 3 * D_MODEL), F32),
        name="ada_mod",
    )(c_all, w_ada, b_ada.reshape(DEPTH, 1, 3 * D_MODEL))


def _layer_kernel(*refs, T, Q, has_cache, emit_v):
    it = iter(refs)
    x_ref = next(it)
    mod_ref = next(it)
    if has_cache:
        ca_prev_ref = next(it)
        cb_prev_ref = next(it)
        ssm_prev_ref = next(it)
    w1_ref = next(it)
    wpa_ref = next(it)
    wpb_ref = next(it)
    wpc_ref = next(it)
    wo_ref = next(it)
    v1024_ref = next(it)
    bm_ref = next(it)
    v512_ref = next(it)
    caw_ref = next(it)
    cbw_ref = next(it)
    dtb_ref = next(it)
    ws_ref = next(it)
    bsf_ref = next(it)
    tri3_ref = next(it)
    um3_ref = next(it)
    lmask_ref = next(it)
    exp3_ref = next(it)
    y_ref = next(it)
    ca_new_ref = next(it)
    cb_new_ref = next(it)
    ssm_new_ref = next(it)
    if emit_v:
        vn_ref = next(it)
    exta, extb, cva, xbcc, dtbs, yss, st = it

    j = pl.program_id(1)
    last = pl.num_programs(1) - 1
    a_off = A_HIST - (CONV_A_WIDTH - 1)
    b_off = B_HIST - (CONV_B_WIDTH - 1)
    half = T // 2

    @pl.when(j == 0)
    def _init():
        exta[:, 0:A_HIST, :] = jnp.zeros((A_SLABS, A_HIST, LANES), F32)
        extb[:, 0:B_HIST, :] = jnp.zeros((B_SLABS, B_HIST, LANES), F32)
        if has_cache:
            for s in range(A_SLABS):
                exta[s, a_off:A_HIST, :] = ca_prev_ref[:, s * LANES:(s + 1) * LANES]
            for s in range(B_SLABS):
                extb[s, b_off:B_HIST, :] = cb_prev_ref[:, s * LANES:(s + 1) * LANES]
            st[...] = ssm_prev_ref[...].T
        else:
            st[...] = jnp.zeros((STATE, D_SSM), F32)

    x = x_ref[...]
    shift = mod_ref[0:1, :]
    scale = mod_ref[1:2, :]
    gate = mod_ref[2:3, :]
    h = _rms(x, v1024_ref[0:1, :]) * (1.0 + scale) + shift
    hb = h.astype(BF16)

    def proj(c0, n):
        return _dot(hb, w1_ref[:, c0:c0 + n])


    def conv_a_slab(s):
        ls = slice(s * LANES, (s + 1) * LANES)
        for par in range(2):
            acc = jnp.broadcast_to(v512_ref[0:1, ls], (half, LANES))
            for k in range(CONV_A_WIDTH):
                acc = acc + caw_ref[k:k + 1, ls] * exta[s, pl.ds(a_off + k + par, half, stride=2), :]
            cva[s, pl.ds(par, half, stride=2), :] = acc

    def conv_b_slab(s):
        ls = slice(s * LANES, (s + 1) * LANES)
        for par in range(2):
            acc = jnp.broadcast_to(cbw_ref[CONV_B_WIDTH:CONV_B_WIDTH + 1, ls], (half, LANES))
            for k in range(CONV_B_WIDTH):
                acc = acc + cbw_ref[k:k + 1, ls] * extb[s, pl.ds(b_off + k + par, half, stride=2), :]
            xbcc[s, pl.ds(par, half, stride=2), :] = _silu(acc)

    filler_cols = ([("c", COL_C + i * PROJ_BLOCK) for i in range(3 * D_C // PROJ_BLOCK)]
                   + [("z", COL_ZB + i * PROJ_BLOCK) for i in range(D_SSM // PROJ_BLOCK)]
                   + [("g", COL_G + i * PROJ_BLOCK) for i in range(N_BRANCH * D_MODEL // PROJ_BLOCK)])
    filled = {"c": [], "z": [], "g": []}

    def filler(n):
        for _ in range(n):
            if not filler_cols:
                return
            kind, c0 = filler_cols.pop(0)
            blk = proj(c0, PROJ_BLOCK)
            if kind == "g":
                b0 = c0 - COL_G
                blk = _sigmoid(blk + bm_ref[:, b0:b0 + PROJ_BLOCK])
            filled[kind].append(blk)

    za_gate = []
    for b in range(D_A // A_BLOCK):
        blk = proj(COL_A + b * 3 * A_BLOCK, 3 * A_BLOCK)
        a_glu = blk[:, 0:A_BLOCK] * _sigmoid(blk[:, A_BLOCK:2 * A_BLOCK])
        za_gate.append(_silu(blk[:, 2 * A_BLOCK:3 * A_BLOCK]))
        for q in range(A_BLOCK // LANES):
            exta[b * (A_BLOCK // LANES) + q, A_HIST:A_HIST + T, :] = a_glu[:, q * LANES:(q + 1) * LANES]
        for q in range(A_BLOCK // LANES):
            conv_a_slab(b * (A_BLOCK // LANES) + q)
            filler(1)
    for s in range(A_SLABS):
        exta[s, a_off:A_HIST, :] = exta[s, T + a_off:T + A_HIST, :]

    b_per_blk = PROJ_BLOCK // LANES
    for i in range(D_XBC // PROJ_BLOCK):
        blk = proj(COL_XBC + i * PROJ_BLOCK, PROJ_BLOCK)
        for q in range(b_per_blk):
            extb[i * b_per_blk + q, B_HIST:B_HIST + T, :] = blk[:, q * LANES:(q + 1) * LANES]
        for q in range(b_per_blk):
            conv_b_slab(i * b_per_blk + q)
    for s in range(B_SLABS):
        extb[s, b_off:B_HIST, :] = extb[s, T + b_off:T + B_HIST, :]

    dt3 = _softplus(_dot(hb, w1_ref[:, COL_DT:COL_DT + DT_PAD]) + dtb_ref[...])
    p1, p2, p3 = _split3(dt3)
    lane = lax.broadcasted_iota(jnp.int32, (T, DT_PAD), 1)
    dsel = jnp.where(lane < HEADS, p1, jnp.where(lane < 2 * HEADS, p2, p3))
    dtbs[...] = _dot(dsel, exp3_ref[...])

    a_row = -jnp.exp(v1024_ref[4:5, :])
    left = lax.broadcasted_iota(jnp.int32, (Q, LANES), 1) < HEAD_DIM
    pair_pad = SSD_CHUNK - Q

    gw = HEADS_PER_GROUP * HEAD_DIM

    n_chunks = T // Q
    chunks = range(n_chunks)

    xs_all = jnp.concatenate([xbcc[s] for s in range(X_SLABS)], axis=1)
    dtb_all = dtbs[...]
    xdt_all = xs_all * dtb_all
    pieces_all = _split3(dtb_all * a_row)
    filler(1)

    tri3 = tri3_ref[...]
    kpad = SEG_K - DT_SPLIT * Q
    segs, csbs = [], []
    for c in chunks:
        pieces = [p[c * Q:(c + 1) * Q, :] for p in pieces_all]
        if kpad:
            pieces.append(jnp.zeros((kpad, D_SSM), BF16))
        pm = jnp.concatenate(pieces, axis=0)
        segs.append(_dot(tri3, pm * um3_ref[...]))
        csbs.append(_dot(tri3, pm))
    filler(2)

    bgs, cgs, g2s = [], [], []
    for c in chunks:
        for g in range(GROUPS):
            bg = xbcc[X_SLABS + g, c * Q:(c + 1) * Q, :].astype(BF16)
            cg = xbcc[X_SLABS + GROUPS + g, c * Q:(c + 1) * Q, :].astype(BF16)
            if pair_pad:
                zrow = jnp.zeros((pair_pad, STATE), BF16)
                bg2 = jnp.concatenate([bg, zrow, bg, zrow], axis=0)
            else:
                bg2 = jnp.concatenate([bg, bg], axis=0)
            bgs.append(bg)
            cgs.append(cg)
            g2s.append(lax.dot_general(cg, bg2, (((1,), (1,)), ((), ())), preferred_element_type=F32))
    filler(2)

    lms = [jnp.exp(segs[c]) * lmask_ref[...] for c in chunks]
    ecss = [jnp.exp(csbs[c]) for c in chunks]
    clasts = [csbs[c][Q - 1:Q, :] for c in chunks]
    cdecs = [jnp.exp(clasts[c]) for c in chunks]
    xdds = [(xdt_all[c * Q:(c + 1) * Q, :] * jnp.exp(clasts[c] - csbs[c])).astype(BF16) for c in chunks]

    y_diags = []
    for c in chunks:
        xdt = xdt_all[c * Q:(c + 1) * Q, :]
        for g in range(GROUPS):
            parts = []
            for pr in range(HEADS_PER_GROUP // 2):
                l0 = g * gw + pr * LANES
                mp = (g2s[c * GROUPS + g] * lms[c][:, l0:l0 + LANES]).astype(BF16)
                xp = xdt[:, l0:l0 + LANES]
                top = jnp.where(left, xp, 0.0).astype(BF16)
                bot = jnp.where(left, 0.0, xp).astype(BF16)
                if pair_pad:
                    zpad = jnp.zeros((pair_pad, LANES), BF16)
                    rhs = jnp.concatenate([top, zpad, bot, zpad], axis=0)
                else:
                    rhs = jnp.concatenate([top, bot], axis=0)
                parts.append(_dot(mp, rhs))
            y_diags.append(jnp.concatenate(parts, axis=1))
        filler(1)

    upds = []
    for c in chunks:
        for g in range(GROUPS):
            upds.append(lax.dot_general(bgs[c * GROUPS + g], xdds[c][:, g * gw:(g + 1) * gw],
                                        (((0,), (0,)), ((), ())), preferred_element_type=F32))
    filler(2)

    conv_a = jnp.concatenate([cva[s] for s in range(A_SLABS)], axis=1)
    a_out = _silu(_ln(conv_a, v512_ref[1:2, :], v512_ref[2:3, :]))
    o_a = _dot((a_out * jnp.concatenate(za_gate, axis=1)).astype(BF16), wpa_ref[...])

    state = [st[:, g * gw:(g + 1) * gw] for g in range(GROUPS)]
    for c in chunks:
        for g in range(GROUPS):
            gl = g * gw
            y_off = _dot(cgs[c * GROUPS + g], state[g].astype(BF16)) * ecss[c][:, gl:gl + gw]
            yss[c * Q:(c + 1) * Q, gl:gl + gw] = y_diags[c * GROUPS + g] + y_off
            state[g] = state[g] * cdecs[c][:, gl:gl + gw] + upds[c * GROUPS + g]
        filler(1)
    for g in range(GROUPS):
        st[:, g * gw:(g + 1) * gw] = state[g]
    filler(len(filler_cols))

    pc = jnp.concatenate(filled["c"], axis=1)
    zb = jnp.concatenate(filled["z"], axis=1)
    gates = jnp.concatenate(filled["g"], axis=1)

    m = gates[:, 0:D_MODEL] * o_a

    y_ssm = yss[...] + xs_all * v1024_ref[3:4, :]
    o_b = _dot(_rms(y_ssm * _silu(zb), v1024_ref[2:3, :]).astype(BF16), wpb_ref[...])
    m = m + gates[:, D_MODEL:2 * D_MODEL] * o_b

    vn = _ln(pc[:, D_C:2 * D_C], v512_ref[3:4, :], v512_ref[4:5, :])
    if emit_v:
        vn_ref[...] = vn
    vnb = vn.astype(BF16)
    tc = min(T, GMLP_CHUNK)
    tri = (lax.broadcasted_iota(jnp.int32, (GMLP_CHUNK, GMLP_CHUNK), 0)
           >= lax.broadcasted_iota(jnp.int32, (GMLP_CHUNK, GMLP_CHUNK), 1))
    ws_b = [jnp.where(tri, ws_ref[g], 0.0).astype(BF16)[0:tc, :] for g in range(GMLP_GROUPS)]
    rows = []
    for cc in range(T // tc):
        cols = []
        for g in range(GMLP_GROUPS):
            vg = vnb[cc * tc:(cc + 1) * tc, g * GMLP_GROUP_DIM:(g + 1) * GMLP_GROUP_DIM]
            if tc < GMLP_CHUNK:
                vg = jnp.concatenate([vg, jnp.zeros((GMLP_CHUNK - tc, GMLP_GROUP_DIM), BF16)], axis=0)
            cols.append(_dot(ws_b[g], vg))
        rows.append(jnp.concatenate(cols, axis=1) + bsf_ref[0:tc, :])
    s = rows[0] if len(rows) == 1 else jnp.concatenate(rows, axis=0)
    o_c = _dot((pc[:, 0:D_C] * s * _silu(pc[:, 2 * D_C:3 * D_C])).astype(BF16), wpc_ref[...])
    m = m + gates[:, 2 * D_MODEL:3 * D_MODEL] * o_c

    out = _dot(m.astype(BF16), wo_ref[...])
    y_ref[...] = x + gate * _rms(out, v1024_ref[1:2, :])

    @pl.when(j == last)
    def _():
        ca_new_ref[...] = jnp.concatenate([exta[s, a_off:A_HIST, :] for s in range(A_SLABS)], axis=1)
        cb_new_ref[...] = jnp.concatenate([extb[s, b_off:B_HIST, :] for s in range(B_SLABS)], axis=1)
        ssm_new_ref[...] = st[...].T


def _ssd_constants(Q):
    l = np.arange(Q)[:, None]
    t = np.arange(Q)[None, :]
    tri = (t <= l).astype(np.float32)
    kpad = SEG_K - DT_SPLIT * Q
    tri3 = np.concatenate([tri] * DT_SPLIT + [np.zeros((Q, kpad), np.float32)], axis=1)
    s_of_lane = (np.arange(D_SSM) % HEAD_DIM)[None, :]
    um = (np.arange(Q)[:, None] > s_of_lane).astype(np.float32)
    um3 = np.concatenate([um] * DT_SPLIT + [np.zeros((kpad, D_SSM), np.float32)], axis=0)
    lmask = (np.arange(Q)[:, None] >= s_of_lane).astype(np.float32)
    lmask = lmask * (s_of_lane < Q)
    head_of_lane = np.arange(D_SSM) // HEAD_DIM
    exp3 = np.zeros((DT_PAD, D_SSM), np.float32)
    for i in range(DT_SPLIT):
        exp3[i * HEADS + head_of_lane, np.arange(D_SSM)] = 1.0
    return (jnp.asarray(tri3, BF16), jnp.asarray(um3, BF16),
            jnp.asarray(lmask, F32), jnp.asarray(exp3, BF16))


def _resident(shape, index_map):
    return pl.BlockSpec(shape, index_map, pipeline_mode=pl.Buffered(1))


def _layer_call(l, x, mod, mod_off, caches, packed, *, T, Q, emit_v):
    B, L, _ = x.shape
    has_cache = caches is not None
    nT = L // T
    (w1, wpa, wpb, wpc, wo, v1024, bm, v512, caw, cbw, dtb, ws, bsf) = packed
    consts = _ssd_constants(Q)

    def lay(*tail):
        return lambda b, j: (l,) + tail

    in_specs = [
        pl.BlockSpec((None, T, D_MODEL), lambda b, j: (b, j, 0)),
        pl.BlockSpec((None, None, 3, D_MODEL), lambda b, j: (l, mod_off + b, 0, 0)),
    ]
    args = [x, mod]
    if has_cache:
        ca_prev, cb_prev, ssm_prev = caches
        in_specs += [
            pl.BlockSpec((None, None, CONV_A_WIDTH - 1, D_A), lambda b, j: (l, b, 0, 0)),
            pl.BlockSpec((None, None, CONV_B_WIDTH - 1, D_XBC), lambda b, j: (l, b, 0, 0)),
            pl.BlockSpec((None, None, D_SSM, STATE), lambda b, j: (l, b, 0, 0)),
        ]
        args += [ca_prev, cb_prev, ssm_prev]
    in_specs += [
        _resident((None, D_MODEL, W1_COLS), lay(0, 0)),
        _resident((None, D_A, D_MODEL), lay(0, 0)),
        _resident((None, D_SSM, D_MODEL), lay(0, 0)),
        _resident((None, D_C, D_MODEL), lay(0, 0)),
        _resident((None, D_MODEL, D_MODEL), lay(0, 0)),
        _resident((None, SUBLANES, D_MODEL), lay(0, 0)),
        _resident((None, 1, N_BRANCH * D_MODEL), lay(0, 0)),
        _resident((None, SUBLANES, D_A), lay(0, 0)),
        _resident((None, 32, D_A), lay(0, 0)),
        _resident((None, SUBLANES, D_XBC), lay(0, 0)),
        _resident((None, 1, DT_PAD), lay(0, 0)),
        _resident((None, GMLP_GROUPS, GMLP_CHUNK, GMLP_CHUNK), lay(0, 0, 0)),
        _resident((None, GMLP_CHUNK, D_C), lay(0, 0)),
    ]
    args += [w1, wpa, wpb, wpc, wo, v1024, bm, v512, caw, cbw, dtb, ws, bsf]
    for cst in consts:
        in_specs.append(_resident(cst.shape, lambda b, j: (0, 0)))
        args.append(cst)

    out_specs = [
        pl.BlockSpec((None, T, D_MODEL), lambda b, j: (b, j, 0)),
        pl.BlockSpec((None, CONV_A_WIDTH - 1, D_A), lambda b, j: (b, 0, 0)),
        pl.BlockSpec((None, CONV_B_WIDTH - 1, D_XBC), lambda b, j: (b, 0, 0)),
        pl.BlockSpec((None, D_SSM, STATE), lambda b, j: (b, 0, 0)),
    ]
    out_shape = [
        jax.ShapeDtypeStruct((B, L, D_MODEL), F32),
        jax.ShapeDtypeStruct((B, CONV_A_WIDTH - 1, D_A), F32),
        jax.ShapeDtypeStruct((B, CONV_B_WIDTH - 1, D_XBC), F32),
        jax.ShapeDtypeStruct((B, D_SSM, STATE), F32),
    ]
    if emit_v:
        out_specs.append(pl.BlockSpec((None, T, D_C), lambda b, j: (b, j, 0)))
        out_shape.append(jax.ShapeDtypeStruct((B, L, D_C), F32))

    scratch = [
        pltpu.VMEM((A_SLABS, A_HIST + T, LANES), F32),
        pltpu.VMEM((B_SLABS, B_HIST + T, LANES), F32),
        pltpu.VMEM((A_SLABS, T, LANES), F32),
        pltpu.VMEM((B_SLABS, T, LANES), F32),
        pltpu.VMEM((T, D_SSM), F32),
        pltpu.VMEM((T, D_SSM), F32),
        pltpu.VMEM((STATE, D_SSM), F32),
    ]
    return pl.pallas_call(
        functools.partial(_layer_kernel, T=T, Q=Q, has_cache=has_cache, emit_v=emit_v),
        grid=(B, nT),
        in_specs=in_specs,
        out_specs=out_specs,
        out_shape=out_shape,
        scratch_shapes=scratch,
        compiler_params=pltpu.CompilerParams(
            dimension_semantics=("arbitrary", "arbitrary"),
            vmem_limit_bytes=VMEM_LIMIT_BYTES),
        name="layer_cached" if has_cache else "layer_fresh",
    )(*args)


def _pack_params(w_in, b_merge, g_pre, g_post, norm_b_g, d_skip, a_log, conv_a_w, conv_a_b, ln_a_g,
                 ln_a_b, ln_c_g, ln_c_b, conv_b_w, conv_b_b, dt_bias, w_pa, w_pb, w_pc, w_o, w_s, b_s):
    splits = np.cumsum([0, D_A, D_A, D_A, D_SSM, D_XBC, HEADS, D_C, D_C, D_C, N_BRANCH * D_MODEL])
    seg = lambda i, k: w_in[:, :, int(splits[i]):int(splits[k])]
    w_dt = seg(5, 6)
    a_blocks = [seg(i, i + 1)[:, :, b * A_BLOCK:(b + 1) * A_BLOCK]
                for b in range(D_A // A_BLOCK) for i in range(3)]
    w1 = jnp.concatenate(
        a_blocks + [seg(3, 4), seg(4, 5), seg(6, 9), seg(9, 10)] + [w_dt] * DT_SPLIT
        + [jnp.zeros((DEPTH, D_MODEL, DT_PAD - DT_SPLIT * HEADS), F32)], axis=-1).astype(BF16)

    def table(rows, width, n_rows):
        t = jnp.stack(rows, axis=1)
        return jnp.pad(t, ((0, 0), (0, n_rows - len(rows)), (0, 0)))

    rep = lambda v: jnp.repeat(v, HEAD_DIM, axis=-1)
    v1024 = table([g_pre, g_post, norm_b_g, rep(d_skip), rep(a_log)], D_MODEL, SUBLANES)
    v512 = table([conv_a_b, ln_a_g, ln_a_b, ln_c_g, ln_c_b], D_A, SUBLANES)
    caw = jnp.pad(conv_a_w, ((0, 0), (0, 32 - CONV_A_WIDTH), (0, 0)))
    cbw = jnp.pad(jnp.concatenate([conv_b_w, conv_b_b[:, None, :]], axis=1),
                  ((0, 0), (0, SUBLANES - CONV_B_WIDTH - 1), (0, 0)))
    dtb = jnp.pad(jnp.concatenate([dt_bias] * DT_SPLIT, axis=-1),
                  ((0, 0), (0, DT_PAD - DT_SPLIT * HEADS)))[:, None, :]
    bsf = jnp.repeat(jnp.swapaxes(b_s, 1, 2), GMLP_GROUP_DIM, axis=-1)
    return (w1, w_pa.astype(BF16), w_pb.astype(BF16), w_pc.astype(BF16), w_o.astype(BF16),
            v1024, b_merge[:, None, :], v512, caw, cbw, dtb, w_s, bsf)


def kernel(x_prompt, x_sample, cache_conv_a, cache_conv_b, state_ssm, c_prompt, c_sample, w_ada, b_ada, g_pre, g_post, w_in, b_merge, conv_a_w, conv_a_b, ln_a_g, ln_a_b, w_pa, conv_b_w, conv_b_b, dt_bias, a_log, d_skip, norm_b_g, w_pb, ln_c_g, ln_c_b, w_s, b_s, w_pc, w_o):
    bp = x_prompt.shape[0]
    bs, ls = x_sample.shape[0], x_sample.shape[1]
    mod = _ada_mod(jnp.concatenate([c_prompt, c_sample], axis=0), w_ada, b_ada)
    mod = mod.reshape(DEPTH, bp + bs, 3, D_MODEL)
    packed = _pack_params(w_in, b_merge, g_pre, g_post, norm_b_g, d_skip, a_log, conv_a_w, conv_a_b,
                          ln_a_g, ln_a_b, ln_c_g, ln_c_b, conv_b_w, conv_b_b, dt_bias, w_pa, w_pb,
                          w_pc, w_o, w_s, b_s)
    caches = (cache_conv_a, cache_conv_b,
              state_ssm.reshape(DEPTH, bs, D_SSM, STATE))

    xp, xs = x_prompt, x_sample
    outs_p, outs_s = [], []
    for l in range(DEPTH):
        xp, ca, cb, ss = _layer_call(l, xp, mod, 0, None, packed, T=PROMPT_TILE, Q=SSD_CHUNK, emit_v=False)
        outs_p.append((ca, cb, ss))
        xs, ca, cb, ss, vn = _layer_call(l, xs, mod, bp, caches, packed, T=ls, Q=min(ls, SSD_CHUNK), emit_v=True)
        outs_s.append((ca, cb, ss, vn))

    stack = lambda outs, i: jnp.stack([o[i] for o in outs])
    ssm_shape = lambda b: (DEPTH, b, HEADS, HEAD_DIM, STATE)
    return (xp, xs,
            stack(outs_p, 0), stack(outs_p, 1), stack(outs_p, 2).reshape(ssm_shape(bp)),
            stack(outs_s, 0), stack(outs_s, 1), stack(outs_s, 2).reshape(ssm_shape(bs)),
            stack(outs_s, 3))


PROMPT_TILE = 256
```

```python
import functools

import numpy as np
import jax
import jax.numpy as jnp
from jax import lax
from jax.experimental import pallas as pl
from jax.experimental.pallas import tpu as pltpu

D_MODEL = 1024
DEPTH = 4
D_A = 512
CONV_A_WIDTH = 31
D_SSM = 1024
HEAD_DIM = 64
HEADS = 16
GROUPS = 4
HEADS_PER_GROUP = HEADS // GROUPS
STATE = 128
CONV_B_WIDTH = 4
D_XBC = D_SSM + 2 * GROUPS * STATE
D_C = 512
GMLP_GROUPS = 4
GMLP_GROUP_DIM = D_C // GMLP_GROUPS
GMLP_CHUNK = 128
SSD_CHUNK = 64
N_BRANCH = 3
RMS_EPS = 1e-6
LN_EPS = 1e-5

LANES = 128
SUBLANES = 8
PROMPT_TILE = 256
ROWS_PER_STEP = 128

COL_A = 0
COL_ZB = COL_A + 3 * D_A
COL_XBC = COL_ZB + D_SSM
COL_C = COL_XBC + D_XBC
COL_G = COL_C + 3 * D_C
COL_DT = COL_G + N_BRANCH * D_MODEL
DT_PAD = LANES
W1_COLS = COL_DT + DT_PAD
DT_SPLIT = 3
SEG_K = 256

A_HIST = 32
B_HIST = 8
A_SLABS = D_A // LANES
B_SLABS = D_XBC // LANES
X_SLABS = D_SSM // LANES
PROJ_BLOCK = 512
A_BLOCK = 256

VMEM_LIMIT_BYTES = 58 * 1024 * 1024

F32 = jnp.float32
BF16 = jnp.bfloat16


def _dot(a, b):
    return jnp.dot(a, b, preferred_element_type=F32)


def _sigmoid(x):
    return 1.0 / (1.0 + jnp.exp(-x))


def _silu(x):
    return x * _sigmoid(x)


def _softplus(x):
    return jnp.maximum(x, 0.0) + jnp.log1p(jnp.exp(-jnp.abs(x)))


def _split3(x):
    p1 = x.astype(BF16)
    r1 = x - p1.astype(F32)
    p2 = r1.astype(BF16)
    r2 = r1 - p2.astype(F32)
    p3 = r2.astype(BF16)
    return p1, p2, p3


def _rms(x, g):
    ms = jnp.mean(x * x, axis=-1, keepdims=True)
    return x * lax.rsqrt(ms + RMS_EPS) * g


def _ln(x, g, b):
    mu = jnp.mean(x, axis=-1, keepdims=True)
    xc = x - mu
    var = jnp.mean(xc * xc, axis=-1, keepdims=True)
    return xc * lax.rsqrt(var + LN_EPS) * g + b


def _rows(parts):
    return parts[0] if len(parts) == 1 else jnp.concatenate(parts, axis=0)


def _ada_kernel(c_ref, w_ref, b_ref, o_ref):
    c = _silu(c_ref[...]).astype(BF16)
    o_ref[...] = _dot(c, w_ref[...].astype(BF16)) + b_ref[...]


def _ada_mod(c_all, w_ada, b_ada):
    nb = c_all.shape[0]
    return pl.pallas_call(
        _ada_kernel,
        grid=(DEPTH, 3),
        in_specs=[
            pl.BlockSpec((nb, D_MODEL), lambda l, n: (0, 0)),
            pl.BlockSpec((None, D_MODEL, D_MODEL), lambda l, n: (l, 0, n)),
            pl.BlockSpec((None, 1, D_MODEL), lambda l, n: (l, 0, n)),
        ],
        out_specs=pl.BlockSpec((None, nb, D_MODEL), lambda l, n: (l, 0, n)),
        out_shape=jax.ShapeDtypeStruct((DEPTH, nb, 3 * D_MODEL), F32),
        name="ada_mod",
    )(c_all, w_ada, b_ada.reshape(DEPTH, 1, 3 * D_MODEL))


def _layer_kernel(*refs, T, Q, NS, has_cache, emit_v):
    it = iter(refs)
    x_ref = next(it)
    mod_ref = next(it)
    if has_cache:
        ca_prev_ref = next(it)
        cb_prev_ref = next(it)
        ssm_prev_ref = next(it)
    w1_ref = next(it)
    wpa_ref = next(it)
    wpb_ref = next(it)
    wpc_ref = next(it)
    wo_ref = next(it)
    v1024_ref = next(it)
    bm_ref = next(it)
    v512_ref = next(it)
    caw_ref = next(it)
    cbw_ref = next(it)
    dtb_ref = next(it)
    ws_ref = next(it)
    bsf_ref = next(it)
    tri3_ref = next(it)
    um3_ref = next(it)
    lmask_ref = next(it)
    exp3_ref = next(it)
    y_ref = next(it)
    ca_new_ref = next(it)
    cb_new_ref = next(it)
    ssm_new_ref = next(it)
    if emit_v:
        vn_ref = next(it)
    exta, extb, cva, xbcc, dtbs, yss, st = it

    j = pl.program_id(1)
    last = pl.num_programs(1) - 1
    a_off = A_HIST - (CONV_A_WIDTH - 1)
    b_off = B_HIST - (CONV_B_WIDTH - 1)
    half = T // 2
    seqs = range(NS)

    def seq_rows(v, n):
        return v[n * T:(n + 1) * T, :]

    def mod_row(k):
        if NS == 1:
            return mod_ref[0, k:k + 1, :]
        return jnp.concatenate([jnp.broadcast_to(mod_ref[n, k:k + 1, :], (T, D_MODEL)) for n in seqs], axis=0)

    @pl.when(j == 0)
    def _init():
        exta[:, 0:A_HIST, :] = jnp.zeros((NS * A_SLABS, A_HIST, LANES), F32)
        extb[:, 0:B_HIST, :] = jnp.zeros((NS * B_SLABS, B_HIST, LANES), F32)
        if has_cache:
            for n in seqs:
                for s in range(A_SLABS):
                    exta[n * A_SLABS + s, a_off:A_HIST, :] = ca_prev_ref[n, :, s * LANES:(s + 1) * LANES]
                for s in range(B_SLABS):
                    extb[n * B_SLABS + s, b_off:B_HIST, :] = cb_prev_ref[n, :, s * LANES:(s + 1) * LANES]
                st[n] = ssm_prev_ref[n].T
        else:
            st[...] = jnp.zeros((NS, STATE, D_SSM), F32)

    x = _rows([x_ref[n] for n in seqs])
    gate = mod_row(2)
    h = _rms(x, v1024_ref[0:1, :]) * (1.0 + mod_row(1)) + mod_row(0)
    hb = h.astype(BF16)

    def proj(c0, n):
        return _dot(hb, w1_ref[:, c0:c0 + n])


    def conv_a_slab(n, s):
        ls = slice(s * LANES, (s + 1) * LANES)
        sl = n * A_SLABS + s
        for par in range(2):
            acc = jnp.broadcast_to(v512_ref[0:1, ls], (half, LANES))
            for k in range(CONV_A_WIDTH):
                acc = acc + caw_ref[k:k + 1, ls] * exta[sl, pl.ds(a_off + k + par, half, stride=2), :]
            cva[sl, pl.ds(par, half, stride=2), :] = acc

    def conv_b_slab(n, s):
        ls = slice(s * LANES, (s + 1) * LANES)
        sl = n * B_SLABS + s
        for par in range(2):
            acc = jnp.broadcast_to(cbw_ref[CONV_B_WIDTH:CONV_B_WIDTH + 1, ls], (half, LANES))
            for k in range(CONV_B_WIDTH):
                acc = acc + cbw_ref[k:k + 1, ls] * extb[sl, pl.ds(b_off + k + par, half, stride=2), :]
            xbcc[sl, pl.ds(par, half, stride=2), :] = _silu(acc)

    filler_cols = ([("c", COL_C + i * PROJ_BLOCK) for i in range(3 * D_C // PROJ_BLOCK)]
                   + [("z", COL_ZB + i * PROJ_BLOCK) for i in range(D_SSM // PROJ_BLOCK)]
                   + [("g", COL_G + i * PROJ_BLOCK) for i in range(N_BRANCH * D_MODEL // PROJ_BLOCK)])
    filled = {"c": [], "z": [], "g": []}

    def filler(n):
        for _ in range(n):
            if not filler_cols:
                return
            kind, c0 = filler_cols.pop(0)
            blk = proj(c0, PROJ_BLOCK)
            if kind == "g":
                b0 = c0 - COL_G
                blk = _sigmoid(blk + bm_ref[:, b0:b0 + PROJ_BLOCK])
            filled[kind].append(blk)

    za_gate = []
    a_per_blk = A_BLOCK // LANES
    for b in range(D_A // A_BLOCK):
        blk = proj(COL_A + b * 3 * A_BLOCK, 3 * A_BLOCK)
        a_glu = blk[:, 0:A_BLOCK] * _sigmoid(blk[:, A_BLOCK:2 * A_BLOCK])
        za_gate.append(_silu(blk[:, 2 * A_BLOCK:3 * A_BLOCK]))
        for n in seqs:
            for q in range(a_per_blk):
                exta[n * A_SLABS + b * a_per_blk + q, A_HIST:A_HIST + T, :] = (
                    seq_rows(a_glu, n)[:, q * LANES:(q + 1) * LANES])
        for q in range(a_per_blk):
            for n in seqs:
                conv_a_slab(n, b * a_per_blk + q)
            filler(1)
    for sl in range(NS * A_SLABS):
        exta[sl, a_off:A_HIST, :] = exta[sl, T + a_off:T + A_HIST, :]

    b_per_blk = PROJ_BLOCK // LANES
    for i in range(D_XBC // PROJ_BLOCK):
        blk = proj(COL_XBC + i * PROJ_BLOCK, PROJ_BLOCK)
        for n in seqs:
            for q in range(b_per_blk):
                extb[n * B_SLABS + i * b_per_blk + q, B_HIST:B_HIST + T, :] = (
                    seq_rows(blk, n)[:, q * LANES:(q + 1) * LANES])
        for n in seqs:
            for q in range(b_per_blk):
                conv_b_slab(n, i * b_per_blk + q)
    for sl in range(NS * B_SLABS):
        extb[sl, b_off:B_HIST, :] = extb[sl, T + b_off:T + B_HIST, :]

    dt3 = _softplus(_dot(hb, w1_ref[:, COL_DT:COL_DT + DT_PAD]) + dtb_ref[...])
    p1, p2, p3 = _split3(dt3)
    lane = lax.broadcasted_iota(jnp.int32, (NS * T, DT_PAD), 1)
    dsel = jnp.where(lane < HEADS, p1, jnp.where(lane < 2 * HEADS, p2, p3))
    dtbs[...] = _dot(dsel, exp3_ref[...])

    a_row = -jnp.exp(v1024_ref[4:5, :])
    left = lax.broadcasted_iota(jnp.int32, (Q, LANES), 1) < HEAD_DIM
    pair_pad = SSD_CHUNK - Q

    gw = HEADS_PER_GROUP * HEAD_DIM

    per_seq = T // Q
    chunk_ids = [(n, c) for n in seqs for c in range(per_seq)]
    chunks = range(len(chunk_ids))

    xs_all = _rows([jnp.concatenate([xbcc[n * B_SLABS + s] for s in range(X_SLABS)], axis=1)
                    for n in seqs])
    dtb_all = dtbs[...]
    xdt_all = xs_all * dtb_all
    pieces_all = _split3(dtb_all * a_row)
    filler(1)

    tri3 = tri3_ref[...]
    kpad = SEG_K - DT_SPLIT * Q
    segs, csbs = [], []
    for k in chunks:
        pieces = [p[k * Q:(k + 1) * Q, :] for p in pieces_all]
        if kpad:
            pieces.append(jnp.zeros((kpad, D_SSM), BF16))
        pm = jnp.concatenate(pieces, axis=0)
        segs.append(_dot(tri3, pm * um3_ref[...]))
        csbs.append(_dot(tri3, pm))
    filler(2)

    bgs, cgs, g2s = [], [], []
    for n, c in chunk_ids:
        for g in range(GROUPS):
            bg = xbcc[n * B_SLABS + X_SLABS + g, c * Q:(c + 1) * Q, :].astype(BF16)
            cg = xbcc[n * B_SLABS + X_SLABS + GROUPS + g, c * Q:(c + 1) * Q, :].astype(BF16)
            if pair_pad:
                zrow = jnp.zeros((pair_pad, STATE), BF16)
                bg2 = jnp.concatenate([bg, zrow, bg, zrow], axis=0)
            else:
                bg2 = jnp.concatenate([bg, bg], axis=0)
            bgs.append(bg)
            cgs.append(cg)
            g2s.append(lax.dot_general(cg, bg2, (((1,), (1,)), ((), ())), preferred_element_type=F32))
    filler(2)

    lms = [jnp.exp(segs[k]) * lmask_ref[...] for k in chunks]
    ecss = [jnp.exp(csbs[k]) for k in chunks]
    clasts = [csbs[k][Q - 1:Q, :] for k in chunks]
    cdecs = [jnp.exp(clasts[k]) for k in chunks]
    xdds = [(xdt_all[k * Q:(k + 1) * Q, :] * jnp.exp(clasts[k] - csbs[k])).astype(BF16) for k in chunks]

    y_diags = []
    for k in chunks:
        xdt = xdt_all[k * Q:(k + 1) * Q, :]
        for g in range(GROUPS):
            parts = []
            for pr in range(HEADS_PER_GROUP // 2):
                l0 = g * gw + pr * LANES
                mp = (g2s[k * GROUPS + g] * lms[k][:, l0:l0 + LANES]).astype(BF16)
                xp = xdt[:, l0:l0 + LANES]
                top = jnp.where(left, xp, 0.0).astype(BF16)
                bot = jnp.where(left, 0.0, xp).astype(BF16)
                if pair_pad:
                    zpad = jnp.zeros((pair_pad, LANES), BF16)
                    rhs = jnp.concatenate([top, zpad, bot, zpad], axis=0)
                else:
                    rhs = jnp.concatenate([top, bot], axis=0)
                parts.append(_dot(mp, rhs))
            y_diags.append(jnp.concatenate(parts, axis=1))
        filler(1)

    upds = []
    for k in chunks:
        for g in range(GROUPS):
            upds.append(lax.dot_general(bgs[k * GROUPS + g], xdds[k][:, g * gw:(g + 1) * gw],
                                        (((0,), (0,)), ((), ())), preferred_element_type=F32))
    filler(2)

    conv_a = _rows([jnp.concatenate([cva[n * A_SLABS + s] for s in range(A_SLABS)], axis=1) for n in seqs])
    a_out = _silu(_ln(conv_a, v512_ref[1:2, :], v512_ref[2:3, :]))
    o_a = _dot((a_out * jnp.concatenate(za_gate, axis=1)).astype(BF16), wpa_ref[...])

    for n in seqs:
        state = [st[n, :, g * gw:(g + 1) * gw] for g in range(GROUPS)]
        for c in range(per_seq):
            k = n * per_seq + c
            for g in range(GROUPS):
                gl = g * gw
                y_off = _dot(cgs[k * GROUPS + g], state[g].astype(BF16)) * ecss[k][:, gl:gl + gw]
                yss[k * Q:(k + 1) * Q, gl:gl + gw] = y_diags[k * GROUPS + g] + y_off
                state[g] = state[g] * cdecs[k][:, gl:gl + gw] + upds[k * GROUPS + g]
            filler(1)
        for g in range(GROUPS):
            st[n, :, g * gw:(g + 1) * gw] = state[g]
    filler(len(filler_cols))

    pc = jnp.concatenate(filled["c"], axis=1)
    zb = jnp.concatenate(filled["z"], axis=1)
    gates = jnp.concatenate(filled["g"], axis=1)

    m = gates[:, 0:D_MODEL] * o_a

    y_ssm = yss[...] + xs_all * v1024_ref[3:4, :]
    o_b = _dot(_rms(y_ssm * _silu(zb), v1024_ref[2:3, :]).astype(BF16), wpb_ref[...])
    m = m + gates[:, D_MODEL:2 * D_MODEL] * o_b

    vn = _ln(pc[:, D_C:2 * D_C], v512_ref[3:4, :], v512_ref[4:5, :])
    if emit_v:
        for n in seqs:
            vn_ref[n] = seq_rows(vn, n)
    vnb = vn.astype(BF16)
    tc = min(T, GMLP_CHUNK)
    tri = (lax.broadcasted_iota(jnp.int32, (GMLP_CHUNK, GMLP_CHUNK), 0)
           >= lax.broadcasted_iota(jnp.int32, (GMLP_CHUNK, GMLP_CHUNK), 1))
    ws_b = [jnp.where(tri, ws_ref[g], 0.0).astype(BF16)[0:tc, :] for g in range(GMLP_GROUPS)]
    rows = []
    for cc in range(NS * T // tc):
        cols = []
        for g in range(GMLP_GROUPS):
            vg = vnb[cc * tc:(cc + 1) * tc, g * GMLP_GROUP_DIM:(g + 1) * GMLP_GROUP_DIM]
            if tc < GMLP_CHUNK:
                vg = jnp.concatenate([vg, jnp.zeros((GMLP_CHUNK - tc, GMLP_GROUP_DIM), BF16)], axis=0)
            cols.append(_dot(ws_b[g], vg))
        rows.append(jnp.concatenate(cols, axis=1) + bsf_ref[0:tc, :])
    s = _rows(rows)
    o_c = _dot((pc[:, 0:D_C] * s * _silu(pc[:, 2 * D_C:3 * D_C])).astype(BF16), wpc_ref[...])
    m = m + gates[:, 2 * D_MODEL:3 * D_MODEL] * o_c

    out = _dot(m.astype(BF16), wo_ref[...])
    y = x + gate * _rms(out, v1024_ref[1:2, :])
    for n in seqs:
        y_ref[n] = seq_rows(y, n)

    @pl.when(j == last)
    def _():
        for n in seqs:
            ca_new_ref[n] = jnp.concatenate(
                [exta[n * A_SLABS + s, a_off:A_HIST, :] for s in range(A_SLABS)], axis=1)
            cb_new_ref[n] = jnp.concatenate(
                [extb[n * B_SLABS + s, b_off:B_HIST, :] for s in range(B_SLABS)], axis=1)
            ssm_new_ref[n] = st[n].T


def _ssd_constants(Q):
    l = np.arange(Q)[:, None]
    t = np.arange(Q)[None, :]
    tri = (t <= l).astype(np.float32)
    kpad = SEG_K - DT_SPLIT * Q
    tri3 = np.concatenate([tri] * DT_SPLIT + [np.zeros((Q, kpad), np.float32)], axis=1)
    s_of_lane = (np.arange(D_SSM) % HEAD_DIM)[None, :]
    um = (np.arange(Q)[:, None] > s_of_lane).astype(np.float32)
    um3 = np.concatenate([um] * DT_SPLIT + [np.zeros((kpad, D_SSM), np.float32)], axis=0)
    lmask = (np.arange(Q)[:, None] >= s_of_lane).astype(np.float32)
    lmask = lmask * (s_of_lane < Q)
    head_of_lane = np.arange(D_SSM) // HEAD_DIM
    exp3 = np.zeros((DT_PAD, D_SSM), np.float32)
    for i in range(DT_SPLIT):
        exp3[i * HEADS + head_of_lane, np.arange(D_SSM)] = 1.0
    return (jnp.asarray(tri3, BF16), jnp.asarray(um3, BF16),
            jnp.asarray(lmask, F32), jnp.asarray(exp3, BF16))


def _resident(shape, index_map):
    return pl.BlockSpec(shape, index_map, pipeline_mode=pl.Buffered(1))


def _layer_call(l, x, mod, mod_off, caches, packed, *, T, Q, NS, emit_v):
    B, L, _ = x.shape
    has_cache = caches is not None
    nT = L // T
    assert B % NS == 0 and mod_off % NS == 0
    (w1, wpa, wpb, wpc, wo, v1024, bm, v512, caw, cbw, dtb, ws, bsf) = packed
    consts = _ssd_constants(Q)

    def lay(*tail):
        return lambda b, j: (l,) + tail

    in_specs = [
        pl.BlockSpec((NS, T, D_MODEL), lambda b, j: (b, j, 0)),
        pl.BlockSpec((None, NS, 3, D_MODEL), lambda b, j: (l, mod_off // NS + b, 0, 0)),
    ]
    args = [x, mod]
    if has_cache:
        ca_prev, cb_prev, ssm_prev = caches
        in_specs += [
            pl.BlockSpec((None, NS, CONV_A_WIDTH - 1, D_A), lambda b, j: (l, b, 0, 0)),
            pl.BlockSpec((None, NS, CONV_B_WIDTH - 1, D_XBC), lambda b, j: (l, b, 0, 0)),
            pl.BlockSpec((None, NS, D_SSM, STATE), lambda b, j: (l, b, 0, 0)),
        ]
        args += [ca_prev, cb_prev, ssm_prev]
    in_specs += [
        _resident((None, D_MODEL, W1_COLS), lay(0, 0)),
        _resident((None, D_A, D_MODEL), lay(0, 0)),
        _resident((None, D_SSM, D_MODEL), lay(0, 0)),
        _resident((None, D_C, D_MODEL), lay(0, 0)),
        _resident((None, D_MODEL, D_MODEL), lay(0, 0)),
        _resident((None, SUBLANES, D_MODEL), lay(0, 0)),
        _resident((None, 1, N_BRANCH * D_MODEL), lay(0, 0)),
        _resident((None, SUBLANES, D_A), lay(0, 0)),
        _resident((None, 32, D_A), lay(0, 0)),
        _resident((None, SUBLANES, D_XBC), lay(0, 0)),
        _resident((None, 1, DT_PAD), lay(0, 0)),
        _resident((None, GMLP_GROUPS, GMLP_CHUNK, GMLP_CHUNK), lay(0, 0, 0)),
        _resident((None, GMLP_CHUNK, D_C), lay(0, 0)),
    ]
    args += [w1, wpa, wpb, wpc, wo, v1024, bm, v512, caw, cbw, dtb, ws, bsf]
    for cst in consts:
        in_specs.append(_resident(cst.shape, lambda b, j: (0, 0)))
        args.append(cst)

    out_specs = [
        pl.BlockSpec((NS, T, D_MODEL), lambda b, j: (b, j, 0)),
        pl.BlockSpec((NS, CONV_A_WIDTH - 1, D_A), lambda b, j: (b, 0, 0)),
        pl.BlockSpec((NS, CONV_B_WIDTH - 1, D_XBC), lambda b, j: (b, 0, 0)),
        pl.BlockSpec((NS, D_SSM, STATE), lambda b, j: (b, 0, 0)),
    ]
    out_shape = [
        jax.ShapeDtypeStruct((B, L, D_MODEL), F32),
        jax.ShapeDtypeStruct((B, CONV_A_WIDTH - 1, D_A), F32),
        jax.ShapeDtypeStruct((B, CONV_B_WIDTH - 1, D_XBC), F32),
        jax.ShapeDtypeStruct((B, D_SSM, STATE), F32),
    ]
    if emit_v:
        out_specs.append(pl.BlockSpec((NS, T, D_C), lambda b, j: (b, j, 0)))
        out_shape.append(jax.ShapeDtypeStruct((B, L, D_C), F32))

    scratch = [
        pltpu.VMEM((NS * A_SLABS, A_HIST + T, LANES), F32),
        pltpu.VMEM((NS * B_SLABS, B_HIST + T, LANES), F32),
        pltpu.VMEM((NS * A_SLABS, T, LANES), F32),
        pltpu.VMEM((NS * B_SLABS, T, LANES), F32),
        pltpu.VMEM((NS * T, D_SSM), F32),
        pltpu.VMEM((NS * T, D_SSM), F32),
        pltpu.VMEM((NS, STATE, D_SSM), F32),
    ]
    return pl.pallas_call(
        functools.partial(_layer_kernel, T=T, Q=Q, NS=NS, has_cache=has_cache, emit_v=emit_v),
        grid=(B // NS, nT),
        in_specs=in_specs,
        out_specs=out_specs,
        out_shape=out_shape,
        scratch_shapes=scratch,
        compiler_params=pltpu.CompilerParams(
            dimension_semantics=("arbitrary", "arbitrary"),
            vmem_limit_bytes=VMEM_LIMIT_BYTES),
        name="layer_cached" if has_cache else "layer_fresh",
    )(*args)


def _pack_params(w_in, b_merge, g_pre, g_post, norm_b_g, d_skip, a_log, conv_a_w, conv_a_b, ln_a_g,
                 ln_a_b, ln_c_g, ln_c_b, conv_b_w, conv_b_b, dt_bias, w_pa, w_pb, w_pc, w_o, w_s, b_s):
    splits = np.cumsum([0, D_A, D_A, D_A, D_SSM, D_XBC, HEADS, D_C, D_C, D_C, N_BRANCH * D_MODEL])
    seg = lambda i, k: w_in[:, :, int(splits[i]):int(splits[k])]
    w_dt = seg(5, 6)
    a_blocks = [seg(i, i + 1)[:, :, b * A_BLOCK:(b + 1) * A_BLOCK]
                for b in range(D_A // A_BLOCK) for i in range(3)]
    w1 = jnp.concatenate(
        a_blocks + [seg(3, 4), seg(4, 5), seg(6, 9), seg(9, 10)] + [w_dt] * DT_SPLIT
        + [jnp.zeros((DEPTH, D_MODEL, DT_PAD - DT_SPLIT * HEADS), F32)], axis=-1).astype(BF16)

    def table(rows, n_rows):
        t = jnp.stack(rows, axis=1)
        return jnp.pad(t, ((0, 0), (0, n_rows - len(rows)), (0, 0)))

    rep = lambda v: jnp.repeat(v, HEAD_DIM, axis=-1)
    v1024 = table([g_pre, g_post, norm_b_g, rep(d_skip), rep(a_log)], SUBLANES)
    v512 = table([conv_a_b, ln_a_g, ln_a_b, ln_c_g, ln_c_b], SUBLANES)
    caw = jnp.pad(conv_a_w, ((0, 0), (0, 32 - CONV_A_WIDTH), (0, 0)))
    cbw = jnp.pad(jnp.concatenate([conv_b_w, conv_b_b[:, None, :]], axis=1),
                  ((0, 0), (0, SUBLANES - CONV_B_WIDTH - 1), (0, 0)))
    dtb = jnp.pad(jnp.concatenate([dt_bias] * DT_SPLIT, axis=-1),
                  ((0, 0), (0, DT_PAD - DT_SPLIT * HEADS)))[:, None, :]
    bsf = jnp.repeat(jnp.swapaxes(b_s, 1, 2), GMLP_GROUP_DIM, axis=-1)
    return (w1, w_pa.astype(BF16), w_pb.astype(BF16), w_pc.astype(BF16), w_o.astype(BF16),
            v1024, b_merge[:, None, :], v512, caw, cbw, dtb, w_s, bsf)


def kernel(x_prompt, x_sample, cache_conv_a, cache_conv_b, state_ssm, c_prompt, c_sample, w_ada, b_ada, g_pre, g_post, w_in, b_merge, conv_a_w, conv_a_b, ln_a_g, ln_a_b, w_pa, conv_b_w, conv_b_b, dt_bias, a_log, d_skip, norm_b_g, w_pb, ln_c_g, ln_c_b, w_s, b_s, w_pc, w_o):
    bp = x_prompt.shape[0]
    bs, ls = x_sample.shape[0], x_sample.shape[1]
    mod = _ada_mod(jnp.concatenate([c_prompt, c_sample], axis=0), w_ada, b_ada)
    mod = mod.reshape(DEPTH, bp + bs, 3, D_MODEL)
    packed = _pack_params(w_in, b_merge, g_pre, g_post, norm_b_g, d_skip, a_log, conv_a_w, conv_a_b,
                          ln_a_g, ln_a_b, ln_c_g, ln_c_b, conv_b_w, conv_b_b, dt_bias, w_pa, w_pb,
                          w_pc, w_o, w_s, b_s)
    caches = (cache_conv_a, cache_conv_b,
              state_ssm.reshape(DEPTH, bs, D_SSM, STATE))
    ns = max(1, min(bs, ROWS_PER_STEP // ls))

    xp, xs = x_prompt, x_sample
    outs_p, outs_s = [], []
    for l in range(DEPTH):
        xp, ca, cb, ss = _layer_call(l, xp, mod, 0, None, packed, T=PROMPT_TILE, Q=SSD_CHUNK, NS=1,
                                     emit_v=False)
        outs_p.append((ca, cb, ss))
        xs, ca, cb, ss, vn = _layer_call(l, xs, mod, bp, caches, packed, T=ls, Q=min(ls, SSD_CHUNK),
                                         NS=ns, emit_v=True)
        outs_s.append((ca, cb, ss, vn))

    stack = lambda outs, i: jnp.stack([o[i] for o in outs])
    ssm_shape = lambda b: (DEPTH, b, HEADS, HEAD_DIM, STATE)
    return (xp, xs,
            stack(outs_p, 0), stack(outs_p, 1), stack(outs_p, 2).reshape(ssm_shape(bp)),
            stack(outs_s, 0), stack(outs_s, 1), stack(outs_s, 2).reshape(ssm_shape(bs)),
            stack(outs_s, 3))
```

```python
import functools

import numpy as np
import jax
import jax.numpy as jnp
from jax import lax
from jax.experimental import pallas as pl
from jax.experimental.pallas import tpu as pltpu

D_MODEL = 1024
DEPTH = 4
D_A = 512
CONV_A_WIDTH = 31
D_SSM = 1024
HEAD_DIM = 64
HEADS = 16
GROUPS = 4
HEADS_PER_GROUP = HEADS // GROUPS
STATE = 128
CONV_B_WIDTH = 4
D_XBC = D_SSM + 2 * GROUPS * STATE
D_C = 512
GMLP_GROUPS = 4
GMLP_GROUP_DIM = D_C // GMLP_GROUPS
GMLP_CHUNK = 128
SSD_CHUNK = 64
N_BRANCH = 3
RMS_EPS = 1e-6
LN_EPS = 1e-5

LANES = 128
SUBLANES = 8
PROMPT_TILE = 256
PROMPT_TILES_PER_STEP = 2
ROWS_PER_STEP = 128

COL_A = 0
COL_ZB = COL_A + 3 * D_A
COL_XBC = COL_ZB + D_SSM
COL_C = COL_XBC + D_XBC
COL_G = COL_C + 3 * D_C
COL_DT = COL_G + N_BRANCH * D_MODEL
DT_PAD = LANES
W1_COLS = COL_DT + DT_PAD
DT_SPLIT = 3
SEG_K = 256

A_HIST = 32
B_HIST = 8
A_SLABS = D_A // LANES
B_SLABS = D_XBC // LANES
X_SLABS = D_SSM // LANES
PROJ_BLOCK = 512
A_BLOCK = 256

VMEM_LIMIT_BYTES = 58 * 1024 * 1024

F32 = jnp.float32
BF16 = jnp.bfloat16


def _dot(a, b):
    return jnp.dot(a, b, preferred_element_type=F32)


def _sigmoid(x):
    return 1.0 / (1.0 + jnp.exp(-x))


def _silu(x):
    return x * _sigmoid(x)


def _softplus(x):
    return jnp.maximum(x, 0.0) + jnp.log1p(jnp.exp(-jnp.abs(x)))


def _split3(x):
    p1 = x.astype(BF16)
    r1 = x - p1.astype(F32)
    p2 = r1.astype(BF16)
    r2 = r1 - p2.astype(F32)
    p3 = r2.astype(BF16)
    return p1, p2, p3


def _rms(x, g):
    ms = jnp.mean(x * x, axis=-1, keepdims=True)
    return x * lax.rsqrt(ms + RMS_EPS) * g


def _ln(x, g, b):
    mu = jnp.mean(x, axis=-1, keepdims=True)
    xc = x - mu
    var = jnp.mean(xc * xc, axis=-1, keepdims=True)
    return xc * lax.rsqrt(var + LN_EPS) * g + b


def _rows(parts):
    return parts[0] if len(parts) == 1 else jnp.concatenate(parts, axis=0)


def _ada_kernel(c_ref, w_ref, b_ref, o_ref):
    c = _silu(c_ref[...]).astype(BF16)
    o_ref[...] = _dot(c, w_ref[...].astype(BF16)) + b_ref[...]


def _ada_mod(c_all, w_ada, b_ada):
    nb = c_all.shape[0]
    return pl.pallas_call(
        _ada_kernel,
        grid=(DEPTH, 3),
        in_specs=[
            pl.BlockSpec((nb, D_MODEL), lambda l, n: (0, 0)),
            pl.BlockSpec((None, D_MODEL, D_MODEL), lambda l, n: (l, 0, n)),
            pl.BlockSpec((None, 1, D_MODEL), lambda l, n: (l, 0, n)),
        ],
        out_specs=pl.BlockSpec((None, nb, D_MODEL), lambda l, n: (l, 0, n)),
        out_shape=jax.ShapeDtypeStruct((DEPTH, nb, 3 * D_MODEL), F32),
        name="ada_mod",
    )(c_all, w_ada, b_ada.reshape(DEPTH, 1, 3 * D_MODEL))


def _layer_kernel(*refs, T, Q, NS, U, has_cache, emit_v):
    it = iter(refs)
    next(it)
    next(it)
    if has_cache:
        ca_prev_ref = next(it)
        cb_prev_ref = next(it)
        ssm_prev_ref = next(it)
    for _ in range(17):
        next(it)
    next(it)
    ca_new_ref = next(it)
    cb_new_ref = next(it)
    ssm_new_ref = next(it)
    if emit_v:
        next(it)
    exta, extb, _, _, _, _, st = it

    j = pl.program_id(1)
    last = pl.num_programs(1) - 1
    a_off = A_HIST - (CONV_A_WIDTH - 1)
    b_off = B_HIST - (CONV_B_WIDTH - 1)
    seqs = range(NS)

    @pl.when(j == 0)
    def _init():
        exta[:, 0:A_HIST, :] = jnp.zeros((NS * A_SLABS, A_HIST, LANES), F32)
        extb[:, 0:B_HIST, :] = jnp.zeros((NS * B_SLABS, B_HIST, LANES), F32)
        if has_cache:
            for n in seqs:
                for s in range(A_SLABS):
                    exta[n * A_SLABS + s, a_off:A_HIST, :] = ca_prev_ref[n, :, s * LANES:(s + 1) * LANES]
                for s in range(B_SLABS):
                    extb[n * B_SLABS + s, b_off:B_HIST, :] = cb_prev_ref[n, :, s * LANES:(s + 1) * LANES]
                st[n] = ssm_prev_ref[n].T
        else:
            st[...] = jnp.zeros((NS, STATE, D_SSM), F32)

    for u in range(U):
        _tile(u, refs, T=T, Q=Q, NS=NS, has_cache=has_cache, emit_v=emit_v)
    rows = U * T
    for sl in range(NS * A_SLABS):
        exta[sl, a_off:A_HIST, :] = exta[sl, rows + a_off:rows + A_HIST, :]
    for sl in range(NS * B_SLABS):
        extb[sl, b_off:B_HIST, :] = extb[sl, rows + b_off:rows + B_HIST, :]

    @pl.when(j == last)
    def _():
        for n in seqs:
            ca_new_ref[n] = jnp.concatenate(
                [exta[n * A_SLABS + s, a_off:A_HIST, :] for s in range(A_SLABS)], axis=1)
            cb_new_ref[n] = jnp.concatenate(
                [extb[n * B_SLABS + s, b_off:B_HIST, :] for s in range(B_SLABS)], axis=1)
            ssm_new_ref[n] = st[n].T


def _tile(u, refs, *, T, Q, NS, has_cache, emit_v):
    it = iter(refs)
    x_ref = next(it)
    mod_ref = next(it)
    if has_cache:
        ca_prev_ref = next(it)
        cb_prev_ref = next(it)
        ssm_prev_ref = next(it)
    w1_ref = next(it)
    wpa_ref = next(it)
    wpb_ref = next(it)
    wpc_ref = next(it)
    wo_ref = next(it)
    v1024_ref = next(it)
    bm_ref = next(it)
    v512_ref = next(it)
    caw_ref = next(it)
    cbw_ref = next(it)
    dtb_ref = next(it)
    ws_ref = next(it)
    bsf_ref = next(it)
    tri3_ref = next(it)
    um3_ref = next(it)
    lmask_ref = next(it)
    exp3_ref = next(it)
    y_ref = next(it)
    ca_new_ref = next(it)
    cb_new_ref = next(it)
    ssm_new_ref = next(it)
    if emit_v:
        vn_ref = next(it)
    exta, extb, cva, xbcc, dtbs, yss, st = it

    ro = u * T
    a_off = A_HIST - (CONV_A_WIDTH - 1) + ro
    b_off = B_HIST - (CONV_B_WIDTH - 1) + ro
    half = T // 2
    seqs = range(NS)

    def seq_rows(v, n):
        return v[n * T:(n + 1) * T, :]

    def mod_row(k):
        if NS == 1:
            return mod_ref[0, k:k + 1, :]
        return jnp.concatenate([jnp.broadcast_to(mod_ref[n, k:k + 1, :], (T, D_MODEL)) for n in seqs], axis=0)

    x = _rows([x_ref[n, ro:ro + T, :] for n in seqs])
    gate = mod_row(2)
    h = _rms(x, v1024_ref[0:1, :]) * (1.0 + mod_row(1)) + mod_row(0)
    hb = h.astype(BF16)

    def proj(c0, n):
        return _dot(hb, w1_ref[:, c0:c0 + n])


    def conv_a_slab(n, s):
        ls = slice(s * LANES, (s + 1) * LANES)
        sl = n * A_SLABS + s
        for par in range(2):
            acc = jnp.broadcast_to(v512_ref[0:1, ls], (half, LANES))
            for k in range(CONV_A_WIDTH):
                acc = acc + caw_ref[k:k + 1, ls] * exta[sl, pl.ds(a_off + k + par, half, stride=2), :]
            cva[sl, pl.ds(par, half, stride=2), :] = acc

    def conv_b_slab(n, s):
        ls = slice(s * LANES, (s + 1) * LANES)
        sl = n * B_SLABS + s
        for par in range(2):
            acc = jnp.broadcast_to(cbw_ref[CONV_B_WIDTH:CONV_B_WIDTH + 1, ls], (half, LANES))
            for k in range(CONV_B_WIDTH):
                acc = acc + cbw_ref[k:k + 1, ls] * extb[sl, pl.ds(b_off + k + par, half, stride=2), :]
            xbcc[sl, pl.ds(par, half, stride=2), :] = _silu(acc)

    filler_cols = ([("c", COL_C + i * PROJ_BLOCK) for i in range(3 * D_C // PROJ_BLOCK)]
                   + [("z", COL_ZB + i * PROJ_BLOCK) for i in range(D_SSM // PROJ_BLOCK)]
                   + [("g", COL_G + i * PROJ_BLOCK) for i in range(N_BRANCH * D_MODEL // PROJ_BLOCK)])
    filled = {"c": [], "z": [], "g": []}

    def filler(n):
        for _ in range(n):
            if not filler_cols:
                return
            kind, c0 = filler_cols.pop(0)
            blk = proj(c0, PROJ_BLOCK)
            if kind == "g":
                b0 = c0 - COL_G
                blk = _sigmoid(blk + bm_ref[:, b0:b0 + PROJ_BLOCK])
            filled[kind].append(blk)

    za_gate = []
    a_per_blk = A_BLOCK // LANES
    for b in range(D_A // A_BLOCK):
        blk = proj(COL_A + b * 3 * A_BLOCK, 3 * A_BLOCK)
        a_glu = blk[:, 0:A_BLOCK] * _sigmoid(blk[:, A_BLOCK:2 * A_BLOCK])
        za_gate.append(_silu(blk[:, 2 * A_BLOCK:3 * A_BLOCK]))
        for n in seqs:
            for q in range(a_per_blk):
                exta[n * A_SLABS + b * a_per_blk + q, A_HIST + ro:A_HIST + ro + T, :] = (
                    seq_rows(a_glu, n)[:, q * LANES:(q + 1) * LANES])
        for q in range(a_per_blk):
            for n in seqs:
                conv_a_slab(n, b * a_per_blk + q)
            filler(1)

    b_per_blk = PROJ_BLOCK // LANES
    for i in range(D_XBC // PROJ_BLOCK):
        blk = proj(COL_XBC + i * PROJ_BLOCK, PROJ_BLOCK)
        for n in seqs:
            for q in range(b_per_blk):
                extb[n * B_SLABS + i * b_per_blk + q, B_HIST + ro:B_HIST + ro + T, :] = (
                    seq_rows(blk, n)[:, q * LANES:(q + 1) * LANES])
        for n in seqs:
            for q in range(b_per_blk):
                conv_b_slab(n, i * b_per_blk + q)

    dt3 = _softplus(_dot(hb, w1_ref[:, COL_DT:COL_DT + DT_PAD]) + dtb_ref[...])
    p1, p2, p3 = _split3(dt3)
    lane = lax.broadcasted_iota(jnp.int32, (NS * T, DT_PAD), 1)
    dsel = jnp.where(lane < HEADS, p1, jnp.where(lane < 2 * HEADS, p2, p3))
    dtbs[...] = _dot(dsel, exp3_ref[...])

    a_row = -jnp.exp(v1024_ref[4:5, :])
    left = lax.broadcasted_iota(jnp.int32, (Q, LANES), 1) < HEAD_DIM
    pair_pad = SSD_CHUNK - Q

    gw = HEADS_PER_GROUP * HEAD_DIM

    per_seq = T // Q
    chunk_ids = [(n, c) for n in seqs for c in range(per_seq)]
    chunks = range(len(chunk_ids))

    xs_all = _rows([jnp.concatenate([xbcc[n * B_SLABS + s] for s in range(X_SLABS)], axis=1)
                    for n in seqs])
    dtb_all = dtbs[...]
    xdt_all = xs_all * dtb_all
    pieces_all = _split3(dtb_all * a_row)
    filler(1)

    tri3 = tri3_ref[...]
    kpad = SEG_K - DT_SPLIT * Q
    segs, csbs = [], []
    for k in chunks:
        pieces = [p[k * Q:(k + 1) * Q, :] for p in pieces_all]
        if kpad:
            pieces.append(jnp.zeros((kpad, D_SSM), BF16))
        pm = jnp.concatenate(pieces, axis=0)
        segs.append(_dot(tri3, pm * um3_ref[...]))
        csbs.append(_dot(tri3, pm))
    filler(2)

    bgs, cgs, g2s = [], [], []
    for n, c in chunk_ids:
        for g in range(GROUPS):
            bg = xbcc[n * B_SLABS + X_SLABS + g, c * Q:(c + 1) * Q, :].astype(BF16)
            cg = xbcc[n * B_SLABS + X_SLABS + GROUPS + g, c * Q:(c + 1) * Q, :].astype(BF16)
            if pair_pad:
                zrow = jnp.zeros((pair_pad, STATE), BF16)
                bg2 = jnp.concatenate([bg, zrow, bg, zrow], axis=0)
            else:
                bg2 = jnp.concatenate([bg, bg], axis=0)
            bgs.append(bg)
            cgs.append(cg)
            g2s.append(lax.dot_general(cg, bg2, (((1,), (1,)), ((), ())), preferred_element_type=F32))
    filler(2)

    lms = [jnp.exp(segs[k]) * lmask_ref[...] for k in chunks]
    ecss = [jnp.exp(csbs[k]) for k in chunks]
    clasts = [csbs[k][Q - 1:Q, :] for k in chunks]
    cdecs = [jnp.exp(clasts[k]) for k in chunks]
    xdds = [(xdt_all[k * Q:(k + 1) * Q, :] * jnp.exp(clasts[k] - csbs[k])).astype(BF16) for k in chunks]

    y_diags = []
    for k in chunks:
        xdt = xdt_all[k * Q:(k + 1) * Q, :]
        for g in range(GROUPS):
            parts = []
            for pr in range(HEADS_PER_GROUP // 2):
                l0 = g * gw + pr * LANES
                mp = (g2s[k * GROUPS + g] * lms[k][:, l0:l0 + LANES]).astype(BF16)
                xp = xdt[:, l0:l0 + LANES]
                top = jnp.where(left, xp, 0.0).astype(BF16)
                bot = jnp.where(left, 0.0, xp).astype(BF16)
                if pair_pad:
                    zpad = jnp.zeros((pair_pad, LANES), BF16)
                    rhs = jnp.concatenate([top, zpad, bot, zpad], axis=0)
                else:
                    rhs = jnp.concatenate([top, bot], axis=0)
                parts.append(_dot(mp, rhs))
            y_diags.append(jnp.concatenate(parts, axis=1))
        filler(1)

    upds = []
    for k in chunks:
        for g in range(GROUPS):
            upds.append(lax.dot_general(bgs[k * GROUPS + g], xdds[k][:, g * gw:(g + 1) * gw],
                                        (((0,), (0,)), ((), ())), preferred_element_type=F32))
    filler(2)

    conv_a = _rows([jnp.concatenate([cva[n * A_SLABS + s] for s in range(A_SLABS)], axis=1) for n in seqs])
    a_out = _silu(_ln(conv_a, v512_ref[1:2, :], v512_ref[2:3, :]))
    o_a = _dot((a_out * jnp.concatenate(za_gate, axis=1)).astype(BF16), wpa_ref[...])

    for n in seqs:
        state = [st[n, :, g * gw:(g + 1) * gw] for g in range(GROUPS)]
        for c in range(per_seq):
            k = n * per_seq + c
            for g in range(GROUPS):
                gl = g * gw
                y_off = _dot(cgs[k * GROUPS + g], state[g].astype(BF16)) * ecss[k][:, gl:gl + gw]
                yss[k * Q:(k + 1) * Q, gl:gl + gw] = y_diags[k * GROUPS + g] + y_off
                state[g] = state[g] * cdecs[k][:, gl:gl + gw] + upds[k * GROUPS + g]
            filler(1)
        for g in range(GROUPS):
            st[n, :, g * gw:(g + 1) * gw] = state[g]
    filler(len(filler_cols))

    pc = jnp.concatenate(filled["c"], axis=1)
    zb = jnp.concatenate(filled["z"], axis=1)
    gates = jnp.concatenate(filled["g"], axis=1)

    m = gates[:, 0:D_MODEL] * o_a

    y_ssm = yss[...] + xs_all * v1024_ref[3:4, :]
    o_b = _dot(_rms(y_ssm * _silu(zb), v1024_ref[2:3, :]).astype(BF16), wpb_ref[...])
    m = m + gates[:, D_MODEL:2 * D_MODEL] * o_b

    vn = _ln(pc[:, D_C:2 * D_C], v512_ref[3:4, :], v512_ref[4:5, :])
    if emit_v:
        for n in seqs:
            vn_ref[n, ro:ro + T, :] = seq_rows(vn, n)
    vnb = vn.astype(BF16)
    tc = min(T, GMLP_CHUNK)
    tri = (lax.broadcasted_iota(jnp.int32, (GMLP_CHUNK, GMLP_CHUNK), 0)
           >= lax.broadcasted_iota(jnp.int32, (GMLP_CHUNK, GMLP_CHUNK), 1))
    ws_b = [jnp.where(tri, ws_ref[g], 0.0).astype(BF16)[0:tc, :] for g in range(GMLP_GROUPS)]
    rows = []
    for cc in range(NS * T // tc):
        cols = []
        for g in range(GMLP_GROUPS):
            vg = vnb[cc * tc:(cc + 1) * tc, g * GMLP_GROUP_DIM:(g + 1) * GMLP_GROUP_DIM]
            if tc < GMLP_CHUNK:
                vg = jnp.concatenate([vg, jnp.zeros((GMLP_CHUNK - tc, GMLP_GROUP_DIM), BF16)], axis=0)
            cols.append(_dot(ws_b[g], vg))
        rows.append(jnp.concatenate(cols, axis=1) + bsf_ref[0:tc, :])
    s = _rows(rows)
    o_c = _dot((pc[:, 0:D_C] * s * _silu(pc[:, 2 * D_C:3 * D_C])).astype(BF16), wpc_ref[...])
    m = m + gates[:, 2 * D_MODEL:3 * D_MODEL] * o_c

    out = _dot(m.astype(BF16), wo_ref[...])
    y = x + gate * _rms(out, v1024_ref[1:2, :])
    for n in seqs:
        y_ref[n, ro:ro + T, :] = seq_rows(y, n)


def _ssd_constants(Q):
    l = np.arange(Q)[:, None]
    t = np.arange(Q)[None, :]
    tri = (t <= l).astype(np.float32)
    kpad = SEG_K - DT_SPLIT * Q
    tri3 = np.concatenate([tri] * DT_SPLIT + [np.zeros((Q, kpad), np.float32)], axis=1)
    s_of_lane = (np.arange(D_SSM) % HEAD_DIM)[None, :]
    um = (np.arange(Q)[:, None] > s_of_lane).astype(np.float32)
    um3 = np.concatenate([um] * DT_SPLIT + [np.zeros((kpad, D_SSM), np.float32)], axis=0)
    lmask = (np.arange(Q)[:, None] >= s_of_lane).astype(np.float32)
    lmask = lmask * (s_of_lane < Q)
    head_of_lane = np.arange(D_SSM) // HEAD_DIM
    exp3 = np.zeros((DT_PAD, D_SSM), np.float32)
    for i in range(DT_SPLIT):
        exp3[i * HEADS + head_of_lane, np.arange(D_SSM)] = 1.0
    return (jnp.asarray(tri3, BF16), jnp.asarray(um3, BF16),
            jnp.asarray(lmask, F32), jnp.asarray(exp3, BF16))


def _resident(shape, index_map):
    return pl.BlockSpec(shape, index_map, pipeline_mode=pl.Buffered(1))


def _layer_call(l, x, mod, mod_off, caches, packed, *, T, Q, NS, U, emit_v):
    B, L, _ = x.shape
    has_cache = caches is not None
    TB = U * T
    nT = L // TB
    assert B % NS == 0 and mod_off % NS == 0 and L % TB == 0
    (w1, wpa, wpb, wpc, wo, v1024, bm, v512, caw, cbw, dtb, ws, bsf) = packed
    consts = _ssd_constants(Q)

    def lay(*tail):
        return lambda b, j: (l,) + tail

    in_specs = [
        pl.BlockSpec((NS, TB, D_MODEL), lambda b, j: (b, j, 0)),
        pl.BlockSpec((None, NS, 3, D_MODEL), lambda b, j: (l, mod_off // NS + b, 0, 0)),
    ]
    args = [x, mod]
    if has_cache:
        ca_prev, cb_prev, ssm_prev = caches
        in_specs += [
            pl.BlockSpec((None, NS, CONV_A_WIDTH - 1, D_A), lambda b, j: (l, b, 0, 0)),
            pl.BlockSpec((None, NS, CONV_B_WIDTH - 1, D_XBC), lambda b, j: (l, b, 0, 0)),
            pl.BlockSpec((None, NS, D_SSM, STATE), lambda b, j: (l, b, 0, 0)),
        ]
        args += [ca_prev, cb_prev, ssm_prev]
    in_specs += [
        _resident((None, D_MODEL, W1_COLS), lay(0, 0)),
        _resident((None, D_A, D_MODEL), lay(0, 0)),
        _resident((None, D_SSM, D_MODEL), lay(0, 0)),
        _resident((None, D_C, D_MODEL), lay(0, 0)),
        _resident((None, D_MODEL, D_MODEL), lay(0, 0)),
        _resident((None, SUBLANES, D_MODEL), lay(0, 0)),
        _resident((None, 1, N_BRANCH * D_MODEL), lay(0, 0)),
        _resident((None, SUBLANES, D_A), lay(0, 0)),
        _resident((None, 32, D_A), lay(0, 0)),
        _resident((None, SUBLANES, D_XBC), lay(0, 0)),
        _resident((None, 1, DT_PAD), lay(0, 0)),
        _resident((None, GMLP_GROUPS, GMLP_CHUNK, GMLP_CHUNK), lay(0, 0, 0)),
        _resident((None, GMLP_CHUNK, D_C), lay(0, 0)),
    ]
    args += [w1, wpa, wpb, wpc, wo, v1024, bm, v512, caw, cbw, dtb, ws, bsf]
    for cst in consts:
        in_specs.append(_resident(cst.shape, lambda b, j: (0, 0)))
        args.append(cst)

    out_specs = [
        pl.BlockSpec((NS, TB, D_MODEL), lambda b, j: (b, j, 0)),
        pl.BlockSpec((NS, CONV_A_WIDTH - 1, D_A), lambda b, j: (b, 0, 0)),
        pl.BlockSpec((NS, CONV_B_WIDTH - 1, D_XBC), lambda b, j: (b, 0, 0)),
        pl.BlockSpec((NS, D_SSM, STATE), lambda b, j: (b, 0, 0)),
    ]
    out_shape = [
        jax.ShapeDtypeStruct((B, L, D_MODEL), F32),
        jax.ShapeDtypeStruct((B, CONV_A_WIDTH - 1, D_A), F32),
        jax.ShapeDtypeStruct((B, CONV_B_WIDTH - 1, D_XBC), F32),
        jax.ShapeDtypeStruct((B, D_SSM, STATE), F32),
    ]
    if emit_v:
        out_specs.append(pl.BlockSpec((NS, TB, D_C), lambda b, j: (b, j, 0)))
        out_shape.append(jax.ShapeDtypeStruct((B, L, D_C), F32))

    scratch = [
        pltpu.VMEM((NS * A_SLABS, A_HIST + TB, LANES), F32),
        pltpu.VMEM((NS * B_SLABS, B_HIST + TB, LANES), F32),
        pltpu.VMEM((NS * A_SLABS, T, LANES), F32),
        pltpu.VMEM((NS * B_SLABS, T, LANES), F32),
        pltpu.VMEM((NS * T, D_SSM), F32),
        pltpu.VMEM((NS * T, D_SSM), F32),
        pltpu.VMEM((NS, STATE, D_SSM), F32),
    ]
    return pl.pallas_call(
        functools.partial(_layer_kernel, T=T, Q=Q, NS=NS, U=U, has_cache=has_cache, emit_v=emit_v),
        grid=(B // NS, nT),
        in_specs=in_specs,
        out_specs=out_specs,
        out_shape=out_shape,
        scratch_shapes=scratch,
        compiler_params=pltpu.CompilerParams(
            dimension_semantics=("arbitrary", "arbitrary"),
            vmem_limit_bytes=VMEM_LIMIT_BYTES),
        name="layer_cached" if has_cache else "layer_fresh",
    )(*args)


def _pack_params(w_in, b_merge, g_pre, g_post, norm_b_g, d_skip, a_log, conv_a_w, conv_a_b, ln_a_g,
                 ln_a_b, ln_c_g, ln_c_b, conv_b_w, conv_b_b, dt_bias, w_pa, w_pb, w_pc, w_o, w_s, b_s):
    splits = np.cumsum([0, D_A, D_A, D_A, D_SSM, D_XBC, HEADS, D_C, D_C, D_C, N_BRANCH * D_MODEL])
    seg = lambda i, k: w_in[:, :, int(splits[i]):int(splits[k])]
    w_dt = seg(5, 6)
    a_blocks = [seg(i, i + 1)[:, :, b * A_BLOCK:(b + 1) * A_BLOCK]
                for b in range(D_A // A_BLOCK) for i in range(3)]
    w1 = jnp.concatenate(
        a_blocks + [seg(3, 4), seg(4, 5), seg(6, 9), seg(9, 10)] + [w_dt] * DT_SPLIT
        + [jnp.zeros((DEPTH, D_MODEL, DT_PAD - DT_SPLIT * HEADS), F32)], axis=-1).astype(BF16)

    def table(rows, n_rows):
        t = jnp.stack(rows, axis=1)
        return jnp.pad(t, ((0, 0), (0, n_rows - len(rows)), (0, 0)))

    rep = lambda v: jnp.repeat(v, HEAD_DIM, axis=-1)
    v1024 = table([g_pre, g_post, norm_b_g, rep(d_skip), rep(a_log)], SUBLANES)
    v512 = table([conv_a_b, ln_a_g, ln_a_b, ln_c_g, ln_c_b], SUBLANES)
    caw = jnp.pad(conv_a_w, ((0, 0), (0, 32 - CONV_A_WIDTH), (0, 0)))
    cbw = jnp.pad(jnp.concatenate([conv_b_w, conv_b_b[:, None, :]], axis=1),
                  ((0, 0), (0, SUBLANES - CONV_B_WIDTH - 1), (0, 0)))
    dtb = jnp.pad(jnp.concatenate([dt_bias] * DT_SPLIT, axis=-1),
                  ((0, 0), (0, DT_PAD - DT_SPLIT * HEADS)))[:, None, :]
    bsf = jnp.repeat(jnp.swapaxes(b_s, 1, 2), GMLP_GROUP_DIM, axis=-1)
    return (w1, w_pa.astype(BF16), w_pb.astype(BF16), w_pc.astype(BF16), w_o.astype(BF16),
            v1024, b_merge[:, None, :], v512, caw, cbw, dtb, w_s, bsf)


def kernel(x_prompt, x_sample, cache_conv_a, cache_conv_b, state_ssm, c_prompt, c_sample, w_ada, b_ada, g_pre, g_post, w_in, b_merge, conv_a_w, conv_a_b, ln_a_g, ln_a_b, w_pa, conv_b_w, conv_b_b, dt_bias, a_log, d_skip, norm_b_g, w_pb, ln_c_g, ln_c_b, w_s, b_s, w_pc, w_o):
    bp = x_prompt.shape[0]
    bs, ls = x_sample.shape[0], x_sample.shape[1]
    mod = _ada_mod(jnp.concatenate([c_prompt, c_sample], axis=0), w_ada, b_ada)
    mod = mod.reshape(DEPTH, bp + bs, 3, D_MODEL)
    packed = _pack_params(w_in, b_merge, g_pre, g_post, norm_b_g, d_skip, a_log, conv_a_w, conv_a_b,
                          ln_a_g, ln_a_b, ln_c_g, ln_c_b, conv_b_w, conv_b_b, dt_bias, w_pa, w_pb,
                          w_pc, w_o, w_s, b_s)
    caches = (cache_conv_a, cache_conv_b,
              state_ssm.reshape(DEPTH, bs, D_SSM, STATE))
    ns = max(1, min(bs, ROWS_PER_STEP // ls))

    xp, xs = x_prompt, x_sample
    outs_p, outs_s = [], []
    for l in range(DEPTH):
        xp, ca, cb, ss = _layer_call(l, xp, mod, 0, None, packed, T=PROMPT_TILE, Q=SSD_CHUNK, NS=1,
                                     U=PROMPT_TILES_PER_STEP, emit_v=False)
        outs_p.append((ca, cb, ss))
        xs, ca, cb, ss, vn = _layer_call(l, xs, mod, bp, caches, packed, T=ls, Q=min(ls, SSD_CHUNK),
                                         NS=ns, U=1, emit_v=True)
        outs_s.append((ca, cb, ss, vn))

    stack = lambda outs, i: jnp.stack([o[i] for o in outs])
    ssm_shape = lambda b: (DEPTH, b, HEADS, HEAD_DIM, STATE)
    return (xp, xs,
            stack(outs_p, 0), stack(outs_p, 1), stack(outs_p, 2).reshape(ssm_shape(bp)),
            stack(outs_s, 0), stack(outs_s, 1), stack(outs_s, 2).reshape(ssm_shape(bs)),
            stack(outs_s, 3))
```

```python
import functools

import numpy as np
import jax
import jax.numpy as jnp
from jax import lax
from jax.experimental import pallas as pl
from jax.experimental.pallas import tpu as pltpu

D_MODEL = 1024
DEPTH = 4
D_A = 512
CONV_A_WIDTH = 31
D_SSM = 1024
HEAD_DIM = 64
HEADS = 16
GROUPS = 4
HEADS_PER_GROUP = HEADS // GROUPS
STATE = 128
CONV_B_WIDTH = 4
D_XBC = D_SSM + 2 * GROUPS * STATE
D_C = 512
GMLP_GROUPS = 4
GMLP_GROUP_DIM = D_C // GMLP_GROUPS
GMLP_CHUNK = 128
SSD_CHUNK = 64
N_BRANCH = 3
RMS_EPS = 1e-6
LN_EPS = 1e-5

LANES = 128
SUBLANES = 8
PROMPT_TILE = 256
ROWS_PER_STEP = 128

COL_A = 0
COL_ZB = COL_A + 3 * D_A
COL_XBC = COL_ZB + D_SSM
COL_C = COL_XBC + D_XBC
COL_G = COL_C + 3 * D_C
COL_DT = COL_G + N_BRANCH * D_MODEL
DT_PAD = LANES
W1_COLS = COL_DT + DT_PAD
DT_SPLIT = 3
SEG_K = 256

A_HIST = 32
B_HIST = 8
A_SLABS = D_A // LANES
B_SLABS = D_XBC // LANES
X_SLABS = D_SSM // LANES
PROJ_BLOCK = 256
A_BLOCK = 256

VMEM_LIMIT_BYTES = 58 * 1024 * 1024

F32 = jnp.float32
BF16 = jnp.bfloat16


def _dot(a, b):
    return jnp.dot(a, b, preferred_element_type=F32)


def _sigmoid(x):
    return 1.0 / (1.0 + jnp.exp(-x))


def _silu(x):
    return x * _sigmoid(x)


def _softplus(x):
    return jnp.maximum(x, 0.0) + jnp.log1p(jnp.exp(-jnp.abs(x)))


def _split3(x):
    p1 = x.astype(BF16)
    r1 = x - p1.astype(F32)
    p2 = r1.astype(BF16)
    r2 = r1 - p2.astype(F32)
    p3 = r2.astype(BF16)
    return p1, p2, p3


def _rms(x, g):
    ms = jnp.mean(x * x, axis=-1, keepdims=True)
    return x * lax.rsqrt(ms + RMS_EPS) * g


def _ln(x, g, b):
    mu = jnp.mean(x, axis=-1, keepdims=True)
    xc = x - mu
    var = jnp.mean(xc * xc, axis=-1, keepdims=True)
    return xc * lax.rsqrt(var + LN_EPS) * g + b


def _rows(parts):
    return parts[0] if len(parts) == 1 else jnp.concatenate(parts, axis=0)


def _ada_kernel(c_ref, w_ref, b_ref, o_ref):
    c = _silu(c_ref[...]).astype(BF16)
    o_ref[...] = _dot(c, w_ref[...].astype(BF16)) + b_ref[...]


def _ada_mod(c_all, w_ada, b_ada):
    nb = c_all.shape[0]
    return pl.pallas_call(
        _ada_kernel,
        grid=(DEPTH, 3),
        in_specs=[
            pl.BlockSpec((nb, D_MODEL), lambda l, n: (0, 0)),
            pl.BlockSpec((None, D_MODEL, D_MODEL), lambda l, n: (l, 0, n)),
            pl.BlockSpec((None, 1, D_MODEL), lambda l, n: (l, 0, n)),
        ],
        out_specs=pl.BlockSpec((None, nb, D_MODEL), lambda l, n: (l, 0, n)),
        out_shape=jax.ShapeDtypeStruct((DEPTH, nb, 3 * D_MODEL), F32),
        name="ada_mod",
    )(c_all, w_ada, b_ada.reshape(DEPTH, 1, 3 * D_MODEL))


def _layer_kernel(*refs, T, Q, NS, has_cache, emit_v):
    it = iter(refs)
    x_ref = next(it)
    mod_ref = next(it)
    if has_cache:
        ca_prev_ref = next(it)
        cb_prev_ref = next(it)
        ssm_prev_ref = next(it)
    w1_ref = next(it)
    wpa_ref = next(it)
    wpb_ref = next(it)
    wpc_ref = next(it)
    wo_ref = next(it)
    v1024_ref = next(it)
    bm_ref = next(it)
    v512_ref = next(it)
    caw_ref = next(it)
    cbw_ref = next(it)
    dtb_ref = next(it)
    ws_ref = next(it)
    bsf_ref = next(it)
    tri3_ref = next(it)
    um3_ref = next(it)
    lmask_ref = next(it)
    exp3_ref = next(it)
    y_ref = next(it)
    ca_new_ref = next(it)
    cb_new_ref = next(it)
    ssm_new_ref = next(it)
    if emit_v:
        vn_ref = next(it)
    exta, extb, cva, xbcc, dtbs, yss, st = it

    j = pl.program_id(1)
    last = pl.num_programs(1) - 1
    a_off = A_HIST - (CONV_A_WIDTH - 1)
    b_off = B_HIST - (CONV_B_WIDTH - 1)
    half = T // 2
    seqs = range(NS)

    def seq_rows(v, n):
        return v[n * T:(n + 1) * T, :]

    def mod_row(k):
        if NS == 1:
            return mod_ref[0, k:k + 1, :]
        return jnp.concatenate([jnp.broadcast_to(mod_ref[n, k:k + 1, :], (T, D_MODEL)) for n in seqs], axis=0)

    @pl.when(j == 0)
    def _init():
        exta[:, 0:A_HIST, :] = jnp.zeros((NS * A_SLABS, A_HIST, LANES), F32)
        extb[:, 0:B_HIST, :] = jnp.zeros((NS * B_SLABS, B_HIST, LANES), F32)
        if has_cache:
            for n in seqs:
                for s in range(A_SLABS):
                    exta[n * A_SLABS + s, a_off:A_HIST, :] = ca_prev_ref[n, :, s * LANES:(s + 1) * LANES]
                for s in range(B_SLABS):
                    extb[n * B_SLABS + s, b_off:B_HIST, :] = cb_prev_ref[n, :, s * LANES:(s + 1) * LANES]
                st[n] = ssm_prev_ref[n].T
        else:
            st[...] = jnp.zeros((NS, STATE, D_SSM), F32)

    x = _rows([x_ref[n] for n in seqs])
    gate = mod_row(2)
    h = _rms(x, v1024_ref[0:1, :]) * (1.0 + mod_row(1)) + mod_row(0)
    hb = h.astype(BF16)

    def proj(c0, n):
        return _dot(hb, w1_ref[:, c0:c0 + n])


    def conv_a_slab(n, s):
        ls = slice(s * LANES, (s + 1) * LANES)
        sl = n * A_SLABS + s
        for par in range(2):
            acc = jnp.broadcast_to(v512_ref[0:1, ls], (half, LANES))
            for k in range(CONV_A_WIDTH):
                acc = acc + caw_ref[k:k + 1, ls] * exta[sl, pl.ds(a_off + k + par, half, stride=2), :]
            cva[sl, pl.ds(par, half, stride=2), :] = acc

    def conv_b_slab(n, s):
        ls = slice(s * LANES, (s + 1) * LANES)
        sl = n * B_SLABS + s
        for par in range(2):
            acc = jnp.broadcast_to(cbw_ref[CONV_B_WIDTH:CONV_B_WIDTH + 1, ls], (half, LANES))
            for k in range(CONV_B_WIDTH):
                acc = acc + cbw_ref[k:k + 1, ls] * extb[sl, pl.ds(b_off + k + par, half, stride=2), :]
            xbcc[sl, pl.ds(par, half, stride=2), :] = _silu(acc)

    filler_cols = ([("c", COL_C + i * PROJ_BLOCK) for i in range(3 * D_C // PROJ_BLOCK)]
                   + [("z", COL_ZB + i * PROJ_BLOCK) for i in range(D_SSM // PROJ_BLOCK)]
                   + [("g", COL_G + i * PROJ_BLOCK) for i in range(N_BRANCH * D_MODEL // PROJ_BLOCK)])
    filled = {"c": [], "z": [], "g": []}

    def filler(n):
        for _ in range(n):
            if not filler_cols:
                return
            kind, c0 = filler_cols.pop(0)
            blk = proj(c0, PROJ_BLOCK)
            if kind == "g":
                b0 = c0 - COL_G
                blk = _sigmoid(blk + bm_ref[:, b0:b0 + PROJ_BLOCK])
            filled[kind].append(blk)

    za_gate = []
    a_per_blk = A_BLOCK // LANES
    for b in range(D_A // A_BLOCK):
        blk = proj(COL_A + b * 3 * A_BLOCK, 3 * A_BLOCK)
        a_glu = blk[:, 0:A_BLOCK] * _sigmoid(blk[:, A_BLOCK:2 * A_BLOCK])
        za_gate.append(_silu(blk[:, 2 * A_BLOCK:3 * A_BLOCK]))
        for n in seqs:
            for q in range(a_per_blk):
                exta[n * A_SLABS + b * a_per_blk + q, A_HIST:A_HIST + T, :] = (
                    seq_rows(a_glu, n)[:, q * LANES:(q + 1) * LANES])
        for q in range(a_per_blk):
            for n in seqs:
                conv_a_slab(n, b * a_per_blk + q)
            filler(1)
    for sl in range(NS * A_SLABS):
        exta[sl, a_off:A_HIST, :] = exta[sl, T + a_off:T + A_HIST, :]

    b_per_blk = PROJ_BLOCK // LANES
    for i in range(D_XBC // PROJ_BLOCK):
        blk = proj(COL_XBC + i * PROJ_BLOCK, PROJ_BLOCK)
        for n in seqs:
            for q in range(b_per_blk):
                extb[n * B_SLABS + i * b_per_blk + q, B_HIST:B_HIST + T, :] = (
                    seq_rows(blk, n)[:, q * LANES:(q + 1) * LANES])
        for n in seqs:
            for q in range(b_per_blk):
                conv_b_slab(n, i * b_per_blk + q)
    for sl in range(NS * B_SLABS):
        extb[sl, b_off:B_HIST, :] = extb[sl, T + b_off:T + B_HIST, :]

    dt3 = _softplus(_dot(hb, w1_ref[:, COL_DT:COL_DT + DT_PAD]) + dtb_ref[...])
    p1, p2, p3 = _split3(dt3)
    lane = lax.broadcasted_iota(jnp.int32, (NS * T, DT_PAD), 1)
    dsel = jnp.where(lane < HEADS, p1, jnp.where(lane < 2 * HEADS, p2, p3))
    dtbs[...] = _dot(dsel, exp3_ref[...])

    a_row = -jnp.exp(v1024_ref[4:5, :])
    left = lax.broadcasted_iota(jnp.int32, (Q, LANES), 1) < HEAD_DIM
    pair_pad = SSD_CHUNK - Q

    gw = HEADS_PER_GROUP * HEAD_DIM

    per_seq = T // Q
    chunk_ids = [(n, c) for n in seqs for c in range(per_seq)]
    chunks = range(len(chunk_ids))

    xs_all = _rows([jnp.concatenate([xbcc[n * B_SLABS + s] for s in range(X_SLABS)], axis=1)
                    for n in seqs])
    dtb_all = dtbs[...]
    xdt_all = xs_all * dtb_all
    pieces_all = _split3(dtb_all * a_row)
    filler(1)

    tri3 = tri3_ref[...]
    kpad = SEG_K - DT_SPLIT * Q
    segs, csbs = [], []
    for k in chunks:
        pieces = [p[k * Q:(k + 1) * Q, :] for p in pieces_all]
        if kpad:
            pieces.append(jnp.zeros((kpad, D_SSM), BF16))
        pm = jnp.concatenate(pieces, axis=0)
        segs.append(_dot(tri3, pm * um3_ref[...]))
        csbs.append(_dot(tri3, pm))
    filler(2)

    bgs, cgs, g2s = [], [], []
    for n, c in chunk_ids:
        for g in range(GROUPS):
            bg = xbcc[n * B_SLABS + X_SLABS + g, c * Q:(c + 1) * Q, :].astype(BF16)
            cg = xbcc[n * B_SLABS + X_SLABS + GROUPS + g, c * Q:(c + 1) * Q, :].astype(BF16)
            if pair_pad:
                zrow = jnp.zeros((pair_pad, STATE), BF16)
                bg2 = jnp.concatenate([bg, zrow, bg, zrow], axis=0)
            else:
                bg2 = jnp.concatenate([bg, bg], axis=0)
            bgs.append(bg)
            cgs.append(cg)
            g2s.append(lax.dot_general(cg, bg2, (((1,), (1,)), ((), ())), preferred_element_type=F32))
    filler(2)

    lms = [jnp.exp(segs[k]) * lmask_ref[...] for k in chunks]
    ecss = [jnp.exp(csbs[k]) for k in chunks]
    clasts = [csbs[k][Q - 1:Q, :] for k in chunks]
    cdecs = [jnp.exp(clasts[k]) for k in chunks]
    xdds = [(xdt_all[k * Q:(k + 1) * Q, :] * jnp.exp(clasts[k] - csbs[k])).astype(BF16) for k in chunks]

    y_diags = []
    for k in chunks:
        xdt = xdt_all[k * Q:(k + 1) * Q, :]
        for g in range(GROUPS):
            parts = []
            for pr in range(HEADS_PER_GROUP // 2):
                l0 = g * gw + pr * LANES
                mp = (g2s[k * GROUPS + g] * lms[k][:, l0:l0 + LANES]).astype(BF16)
                xp = xdt[:, l0:l0 + LANES]
                top = jnp.where(left, xp, 0.0).astype(BF16)
                bot = jnp.where(left, 0.0, xp).astype(BF16)
                if pair_pad:
                    zpad = jnp.zeros((pair_pad, LANES), BF16)
                    rhs = jnp.concatenate([top, zpad, bot, zpad], axis=0)
                else:
                    rhs = jnp.concatenate([top, bot], axis=0)
                parts.append(_dot(mp, rhs))
            y_diags.append(jnp.concatenate(parts, axis=1))
        filler(1)

    upds = []
    for k in chunks:
        for g in range(GROUPS):
            upds.append(lax.dot_general(bgs[k * GROUPS + g], xdds[k][:, g * gw:(g + 1) * gw],
                                        (((0,), (0,)), ((), ())), preferred_element_type=F32))
    filler(2)

    conv_a = _rows([jnp.concatenate([cva[n * A_SLABS + s] for s in range(A_SLABS)], axis=1) for n in seqs])
    a_out = _silu(_ln(conv_a, v512_ref[1:2, :], v512_ref[2:3, :]))
    o_a = _dot((a_out * jnp.concatenate(za_gate, axis=1)).astype(BF16), wpa_ref[...])

    for n in seqs:
        state = [st[n, :, g * gw:(g + 1) * gw] for g in range(GROUPS)]
        for c in range(per_seq):
            k = n * per_seq + c
            for g in range(GROUPS):
                gl = g * gw
                y_off = _dot(cgs[k * GROUPS + g], state[g].astype(BF16)) * ecss[k][:, gl:gl + gw]
                yss[k * Q:(k + 1) * Q, gl:gl + gw] = y_diags[k * GROUPS + g] + y_off
                state[g] = state[g] * cdecs[k][:, gl:gl + gw] + upds[k * GROUPS + g]
            filler(1)
        for g in range(GROUPS):
            st[n, :, g * gw:(g + 1) * gw] = state[g]
    filler(len(filler_cols))

    pc = jnp.concatenate(filled["c"], axis=1)
    zb = jnp.concatenate(filled["z"], axis=1)
    gates = jnp.concatenate(filled["g"], axis=1)

    m = gates[:, 0:D_MODEL] * o_a

    y_ssm = yss[...] + xs_all * v1024_ref[3:4, :]
    o_b = _dot(_rms(y_ssm * _silu(zb), v1024_ref[2:3, :]).astype(BF16), wpb_ref[...])
    m = m + gates[:, D_MODEL:2 * D_MODEL] * o_b

    vn = _ln(pc[:, D_C:2 * D_C], v512_ref[3:4, :], v512_ref[4:5, :])
    if emit_v:
        for n in seqs:
            vn_ref[n] = seq_rows(vn, n)
    vnb = vn.astype(BF16)
    tc = min(T, GMLP_CHUNK)
    tri = (lax.broadcasted_iota(jnp.int32, (GMLP_CHUNK, GMLP_CHUNK), 0)
           >= lax.broadcasted_iota(jnp.int32, (GMLP_CHUNK, GMLP_CHUNK), 1))
    ws_b = [jnp.where(tri, ws_ref[g], 0.0).astype(BF16)[0:tc, :] for g in range(GMLP_GROUPS)]
    rows = []
    for cc in range(NS * T // tc):
        cols = []
        for g in range(GMLP_GROUPS):
            vg = vnb[cc * tc:(cc + 1) * tc, g * GMLP_GROUP_DIM:(g + 1) * GMLP_GROUP_DIM]
            if tc < GMLP_CHUNK:
                vg = jnp.concatenate([vg, jnp.zeros((GMLP_CHUNK - tc, GMLP_GROUP_DIM), BF16)], axis=0)
            cols.append(_dot(ws_b[g], vg))
        rows.append(jnp.concatenate(cols, axis=1) + bsf_ref[0:tc, :])
    s = _rows(rows)
    o_c = _dot((pc[:, 0:D_C] * s * _silu(pc[:, 2 * D_C:3 * D_C])).astype(BF16), wpc_ref[...])
    m = m + gates[:, 2 * D_MODEL:3 * D_MODEL] * o_c

    out = _dot(m.astype(BF16), wo_ref[...])
    y = x + gate * _rms(out, v1024_ref[1:2, :])
    for n in seqs:
        y_ref[n] = seq_rows(y, n)

    @pl.when(j == last)
    def _():
        for n in seqs:
            ca_new_ref[n] = jnp.concatenate(
                [exta[n * A_SLABS + s, a_off:A_HIST, :] for s in range(A_SLABS)], axis=1)
            cb_new_ref[n] = jnp.concatenate(
                [extb[n * B_SLABS + s, b_off:B_HIST, :] for s in range(B_SLABS)], axis=1)
            ssm_new_ref[n] = st[n].T


def _ssd_constants(Q):
    l = np.arange(Q)[:, None]
    t = np.arange(Q)[None, :]
    tri = (t <= l).astype(np.float32)
    kpad = SEG_K - DT_SPLIT * Q
    tri3 = np.concatenate([tri] * DT_SPLIT + [np.zeros((Q, kpad), np.float32)], axis=1)
    s_of_lane = (np.arange(D_SSM) % HEAD_DIM)[None, :]
    um = (np.arange(Q)[:, None] > s_of_lane).astype(np.float32)
    um3 = np.concatenate([um] * DT_SPLIT + [np.zeros((kpad, D_SSM), np.float32)], axis=0)
    lmask = (np.arange(Q)[:, None] >= s_of_lane).astype(np.float32)
    lmask = lmask * (s_of_lane < Q)
    head_of_lane = np.arange(D_SSM) // HEAD_DIM
    exp3 = np.zeros((DT_PAD, D_SSM), np.float32)
    for i in range(DT_SPLIT):
        exp3[i * HEADS + head_of_lane, np.arange(D_SSM)] = 1.0
    return (jnp.asarray(tri3, BF16), jnp.asarray(um3, BF16),
            jnp.asarray(lmask, F32), jnp.asarray(exp3, BF16))


def _resident(shape, index_map):
    return pl.BlockSpec(shape, index_map, pipeline_mode=pl.Buffered(1))


def _layer_call(l, x, mod, mod_off, caches, packed, *, T, Q, NS, emit_v):
    B, L, _ = x.shape
    has_cache = caches is not None
    nT = L // T
    assert B % NS == 0 and mod_off % NS == 0
    (w1, wpa, wpb, wpc, wo, v1024, bm, v512, caw, cbw, dtb, ws, bsf) = packed
    consts = _ssd_constants(Q)

    def lay(*tail):
        return lambda b, j: (l,) + tail

    in_specs = [
        pl.BlockSpec((NS, T, D_MODEL), lambda b, j: (b, j, 0)),
        pl.BlockSpec((None, NS, 3, D_MODEL), lambda b, j: (l, mod_off // NS + b, 0, 0)),
    ]
    args = [x, mod]
    if has_cache:
        ca_prev, cb_prev, ssm_prev = caches
        in_specs += [
            pl.BlockSpec((None, NS, CONV_A_WIDTH - 1, D_A), lambda b, j: (l, b, 0, 0)),
            pl.BlockSpec((None, NS, CONV_B_WIDTH - 1, D_XBC), lambda b, j: (l, b, 0, 0)),
            pl.BlockSpec((None, NS, D_SSM, STATE), lambda b, j: (l, b, 0, 0)),
        ]
        args += [ca_prev, cb_prev, ssm_prev]
    in_specs += [
        _resident((None, D_MODEL, W1_COLS), lay(0, 0)),
        _resident((None, D_A, D_MODEL), lay(0, 0)),
        _resident((None, D_SSM, D_MODEL), lay(0, 0)),
        _resident((None, D_C, D_MODEL), lay(0, 0)),
        _resident((None, D_MODEL, D_MODEL), lay(0, 0)),
        _resident((None, SUBLANES, D_MODEL), lay(0, 0)),
        _resident((None, 1, N_BRANCH * D_MODEL), lay(0, 0)),
        _resident((None, SUBLANES, D_A), lay(0, 0)),
        _resident((None, 32, D_A), lay(0, 0)),
        _resident((None, SUBLANES, D_XBC), lay(0, 0)),
        _resident((None, 1, DT_PAD), lay(0, 0)),
        _resident((None, GMLP_GROUPS, GMLP_CHUNK, GMLP_CHUNK), lay(0, 0, 0)),
        _resident((None, GMLP_CHUNK, D_C), lay(0, 0)),
    ]
    args += [w1, wpa, wpb, wpc, wo, v1024, bm, v512, caw, cbw, dtb, ws, bsf]
    for cst in consts:
        in_specs.append(_resident(cst.shape, lambda b, j: (0, 0)))
        args.append(cst)

    out_specs = [
        pl.BlockSpec((NS, T, D_MODEL), lambda b, j: (b, j, 0)),
        pl.BlockSpec((NS, CONV_A_WIDTH - 1, D_A), lambda b, j: (b, 0, 0)),
        pl.BlockSpec((NS, CONV_B_WIDTH - 1, D_XBC), lambda b, j: (b, 0, 0)),
        pl.BlockSpec((NS, D_SSM, STATE), lambda b, j: (b, 0, 0)),
    ]
    out_shape = [
        jax.ShapeDtypeStruct((B, L, D_MODEL), F32),
        jax.ShapeDtypeStruct((B, CONV_A_WIDTH - 1, D_A), F32),
        jax.ShapeDtypeStruct((B, CONV_B_WIDTH - 1, D_XBC), F32),
        jax.ShapeDtypeStruct((B, D_SSM, STATE), F32),
    ]
    if emit_v:
        out_specs.append(pl.BlockSpec((NS, T, D_C), lambda b, j: (b, j, 0)))
        out_shape.append(jax.ShapeDtypeStruct((B, L, D_C), F32))

    scratch = [
        pltpu.VMEM((NS * A_SLABS, A_HIST + T, LANES), F32),
        pltpu.VMEM((NS * B_SLABS, B_HIST + T, LANES), F32),
        pltpu.VMEM((NS * A_SLABS, T, LANES), F32),
        pltpu.VMEM((NS * B_SLABS, T, LANES), F32),
        pltpu.VMEM((NS * T, D_SSM), F32),
        pltpu.VMEM((NS * T, D_SSM), F32),
        pltpu.VMEM((NS, STATE, D_SSM), F32),
    ]
    return pl.pallas_call(
        functools.partial(_layer_kernel, T=T, Q=Q, NS=NS, has_cache=has_cache, emit_v=emit_v),
        grid=(B // NS, nT),
        in_specs=in_specs,
        out_specs=out_specs,
        out_shape=out_shape,
        scratch_shapes=scratch,
        compiler_params=pltpu.CompilerParams(
            dimension_semantics=("arbitrary", "arbitrary"),
            vmem_limit_bytes=VMEM_LIMIT_BYTES),
        name="layer_cached" if has_cache else "layer_fresh",
    )(*args)


def _pack_params(w_in, b_merge, g_pre, g_post, norm_b_g, d_skip, a_log, conv_a_w, conv_a_b, ln_a_g,
                 ln_a_b, ln_c_g, ln_c_b, conv_b_w, conv_b_b, dt_bias, w_pa, w_pb, w_pc, w_o, w_s, b_s):
    splits = np.cumsum([0, D_A, D_A, D_A, D_SSM, D_XBC, HEADS, D_C, D_C, D_C, N_BRANCH * D_MODEL])
    seg = lambda i, k: w_in[:, :, int(splits[i]):int(splits[k])]
    w_dt = seg(5, 6)
    a_blocks = [seg(i, i + 1)[:, :, b * A_BLOCK:(b + 1) * A_BLOCK]
                for b in range(D_A // A_BLOCK) for i in range(3)]
    w1 = jnp.concatenate(
        a_blocks + [seg(3, 4), seg(4, 5), seg(6, 9), seg(9, 10)] + [w_dt] * DT_SPLIT
        + [jnp.zeros((DEPTH, D_MODEL, DT_PAD - DT_SPLIT * HEADS), F32)], axis=-1).astype(BF16)

    def table(rows, n_rows):
        t = jnp.stack(rows, axis=1)
        return jnp.pad(t, ((0, 0), (0, n_rows - len(rows)), (0, 0)))

    rep = lambda v: jnp.repeat(v, HEAD_DIM, axis=-1)
    v1024 = table([g_pre, g_post, norm_b_g, rep(d_skip), rep(a_log)], SUBLANES)
    v512 = table([conv_a_b, ln_a_g, ln_a_b, ln_c_g, ln_c_b], SUBLANES)
    caw = jnp.pad(conv_a_w, ((0, 0), (0, 32 - CONV_A_WIDTH), (0, 0)))
    cbw = jnp.pad(jnp.concatenate([conv_b_w, conv_b_b[:, None, :]], axis=1),
                  ((0, 0), (0, SUBLANES - CONV_B_WIDTH - 1), (0, 0)))
    dtb = jnp.pad(jnp.concatenate([dt_bias] * DT_SPLIT, axis=-1),
                  ((0, 0), (0, DT_PAD - DT_SPLIT * HEADS)))[:, None, :]
    bsf = jnp.repeat(jnp.swapaxes(b_s, 1, 2), GMLP_GROUP_DIM, axis=-1)
    return (w1, w_pa.astype(BF16), w_pb.astype(BF16), w_pc.astype(BF16), w_o.astype(BF16),
            v1024, b_merge[:, None, :], v512, caw, cbw, dtb, w_s, bsf)


def kernel(x_prompt, x_sample, cache_conv_a, cache_conv_b, state_ssm, c_prompt, c_sample, w_ada, b_ada, g_pre, g_post, w_in, b_merge, conv_a_w, conv_a_b, ln_a_g, ln_a_b, w_pa, conv_b_w, conv_b_b, dt_bias, a_log, d_skip, norm_b_g, w_pb, ln_c_g, ln_c_b, w_s, b_s, w_pc, w_o):
    bp = x_prompt.shape[0]
    bs, ls = x_sample.shape[0], x_sample.shape[1]
    mod = _ada_mod(jnp.concatenate([c_prompt, c_sample], axis=0), w_ada, b_ada)
    mod = mod.reshape(DEPTH, bp + bs, 3, D_MODEL)
    packed = _pack_params(w_in, b_merge, g_pre, g_post, norm_b_g, d_skip, a_log, conv_a_w, conv_a_b,
                          ln_a_g, ln_a_b, ln_c_g, ln_c_b, conv_b_w, conv_b_b, dt_bias, w_pa, w_pb,
                          w_pc, w_o, w_s, b_s)
    caches = (cache_conv_a, cache_conv_b,
              state_ssm.reshape(DEPTH, bs, D_SSM, STATE))
    ns = max(1, min(bs, ROWS_PER_STEP // ls))

    xp, xs = x_prompt, x_sample
    outs_p, outs_s = [], []
    for l in range(DEPTH):
        xp, ca, cb, ss = _layer_call(l, xp, mod, 0, None, packed, T=PROMPT_TILE, Q=SSD_CHUNK, NS=1,
                                     emit_v=False)
        outs_p.append((ca, cb, ss))
        xs, ca, cb, ss, vn = _layer_call(l, xs, mod, bp, caches, packed, T=ls, Q=min(ls, SSD_CHUNK),
                                         NS=ns, emit_v=True)
        outs_s.append((ca, cb, ss, vn))

    stack = lambda outs, i: jnp.stack([o[i] for o in outs])
    ssm_shape = lambda b: (DEPTH, b, HEADS, HEAD_DIM, STATE)
    return (xp, xs,
            stack(outs_p, 0), stack(outs_p, 1), stack(outs_p, 2).reshape(ssm_shape(bp)),
            stack(outs_s, 0), stack(outs_s, 1), stack(outs_s, 2).reshape(ssm_shape(bs)),
            stack(outs_s, 3))
```
